```python
import jax
import jax.numpy as jnp
from jax import lax
import numpy as np

D_MODEL = 1024
BATCH = 16
SEQ = 2048
DEPTH = 1

MEM_LEN = 256
GDN_HEADS = 4
GDN_DK = 128
GDN_DV = 128
GDN_CONV = 4
GDN_CHUNK = 64
SB_HEADS = 8
SB_DH = 64
SB_QBLOCK = 128
MEM_HEADS = 4
MEM_DH = 128
N_BRANCHES = 3
N_EXPERTS = 32
TOP_K = 4
D_FF = D_MODEL
SWIGLU_LIMIT = 7.0
SWIGLU_ALPHA = 1.702
MOE_BLOCK = 128
LN_EPS = 1e-5
NORM_EPS = 1e-6
DEEPNORM_ALPHA = (2 * DEPTH) ** 0.25
DEEPNORM_BETA = (8 * DEPTH) ** -0.25

GDN_QK = GDN_HEADS * GDN_DK
GDN_V = GDN_HEADS * GDN_DV
SB_W = SB_HEADS * SB_DH
MEM_W = MEM_HEADS * MEM_DH
GDN_CONV_CH = 2 * GDN_QK + GDN_V
IN_SIZES = (GDN_QK, GDN_QK, GDN_V, GDN_V, GDN_HEADS, GDN_HEADS, SB_W, SB_W, SB_W, MEM_W, N_BRANCHES * D_MODEL)
D_IN = sum(IN_SIZES)

kernel_name = 'hybrid_gdn_stickbreak_memxattn_moe_deepnorm'


def _layer_norm(x, g, b):
    xf = x.astype(jnp.float32)
    mu = jnp.mean(xf, -1, keepdims=True)
    var = jnp.mean(jnp.square(xf - mu), -1, keepdims=True)
    y = (xf - mu) * lax.rsqrt(var + LN_EPS) * g.astype(jnp.float32) + b.astype(jnp.float32)
    return y.astype(x.dtype)


def _l2norm(t):
    return t * lax.rsqrt(jnp.sum(t * t, -1, keepdims=True) + NORM_EPS)


def _heads(t, n):
    b, s, _ = t.shape
    return t.reshape(b, s, n, -1).transpose(0, 2, 1, 3)


def _causal_depthwise_conv(x, w):
    return lax.conv_general_dilated(
        x, w[:, None, :].astype(x.dtype), window_strides=(1,),
        padding=((w.shape[0] - 1, 0),), dimension_numbers=('NWC', 'WIO', 'NWC'),
        feature_group_count=x.shape[-1])


def _gated_delta_rule(q, k, v, g, beta):
    b, h, s, dk = q.shape
    dv = v.shape[-1]
    n = s // GDN_CHUNK
    q = q.reshape(b, h, n, GDN_CHUNK, dk)
    k = k.reshape(b, h, n, GDN_CHUNK, dk)
    v = v.reshape(b, h, n, GDN_CHUNK, dv)
    g = g.reshape(b, h, n, GDN_CHUNK)
    beta = beta.reshape(b, h, n, GDN_CHUNK)
    gc = jnp.cumsum(g, axis=-1)
    idx = jnp.arange(GDN_CHUNK)
    lower_incl = idx[:, None] >= idx[None, :]
    lower_strict = idx[:, None] > idx[None, :]
    diff = gc[..., :, None] - gc[..., None, :]
    decay = jnp.where(lower_incl, jnp.exp(jnp.where(lower_incl, diff, 0.0)), 0.0)
    kk = jnp.einsum('bhncd,bhnjd->bhncj', k, k)
    a_mat = jnp.where(lower_strict, kk * decay * beta[..., None], 0.0)
    rhs = jnp.concatenate([v * beta[..., None], k * (beta * jnp.exp(gc))[..., None]], axis=-1)
    sol = lax.linalg.triangular_solve(a_mat, rhs, left_side=True, lower=True, unit_diagonal=True)
    u, w = sol[..., :dv], sol[..., dv:]
    attn = jnp.einsum('bhncd,bhnjd->bhncj', q, k) * decay
    g_last = gc[..., -1]
    qg = q * jnp.exp(gc)[..., None]
    kd = k * jnp.exp(g_last[..., None] - gc)[..., None]

    def step(state, xs):
        qg_i, kd_i, u_i, w_i, attn_i, gl_i = xs
        v_new = u_i - jnp.einsum('bhck,bhkv->bhcv', w_i, state)
        o_i = jnp.einsum('bhck,bhkv->bhcv', qg_i, state) + jnp.einsum('bhcj,bhjv->bhcv', attn_i, v_new)
        state = state * jnp.exp(gl_i)[..., None, None] + jnp.einsum('bhck,bhcv->bhkv', kd_i, v_new)
        return state, o_i

    xs = tuple(jnp.moveaxis(t, 2, 0) for t in (qg, kd, u, w, attn, g_last))
    state0 = jnp.zeros((b, h, dk, dv), jnp.float32)
    _, o = lax.scan(step, state0, xs)
    return jnp.moveaxis(o, 0, 2).reshape(b, h, s, dv)


def _stick_breaking_attention(q, k, v):
    s_len, dh = q.shape[2], q.shape[3]
    scale = dh ** -0.5
    outs = []
    for blk in range(s_len // SB_QBLOCK):
        q0 = blk * SB_QBLOCK
        q1 = q0 + SB_QBLOCK
        kb, vb = k[:, :, :q1], v[:, :, :q1]
        z = jnp.einsum('bhqd,bhkd->bhqk', q[:, :, q0:q1], kb).astype(jnp.float32) * scale
        query_pos = q0 + jnp.arange(SB_QBLOCK)
        key_pos = jnp.arange(q1)
        causal = key_pos[None, :] < query_pos[:, None]
        log_stay = jnp.where(causal, jax.nn.log_sigmoid(-z), 0.0)
        log_stay_after = lax.cumsum(log_stay, axis=3, reverse=True) - log_stay
        weights = jnp.where(causal, jnp.exp(jax.nn.log_sigmoid(z) + log_stay_after), 0.0)
        outs.append(jnp.einsum('bhqk,bhkd->bhqd', weights.astype(vb.dtype), vb))
    return jnp.concatenate(outs, axis=2)


def _memory_attention(q, k, v):
    scores = jnp.einsum('bhqd,bhmd->bhqm', q, k).astype(jnp.float32) * (q.shape[-1] ** -0.5)
    p = jax.nn.softmax(scores, axis=-1)
    return jnp.einsum('bhqm,bhmd->bhqd', p.astype(v.dtype), v)


def _moe(x, w_router, b_router, w_gate_up, b_gate_up, w_down, b_down):
    b, s, d = x.shape
    t = b * s
    xt = x.reshape(t, d)
    logits = (xt @ w_router).astype(jnp.float32) + b_router.astype(jnp.float32)
    top_val, top_idx = lax.top_k(logits, TOP_K)
    gate = jax.nn.softmax(top_val, axis=-1)
    n_assign = t * TOP_K
    expert_flat = top_idx.reshape(n_assign)
    token_flat = jnp.repeat(jnp.arange(t, dtype=jnp.int32), TOP_K)
    gate_flat = gate.reshape(n_assign)
    order = jnp.argsort(expert_flat)
    e_sorted = expert_flat[order]
    t_sorted = token_flat[order]
    g_sorted = gate_flat[order]
    counts = jnp.bincount(expert_flat, length=N_EXPERTS)
    padded = (counts + MOE_BLOCK - 1) // MOE_BLOCK * MOE_BLOCK
    group_start = jnp.cumsum(counts) - counts
    padded_end = jnp.cumsum(padded)
    padded_start = padded_end - padded
    dst = padded_start[e_sorted] + jnp.arange(n_assign, dtype=jnp.int32) - group_start[e_sorted]
    n_blocks = -(-n_assign // MOE_BLOCK) + N_EXPERTS
    p_len = n_blocks * MOE_BLOCK
    tok_pad = jnp.full((p_len,), t, jnp.int32).at[dst].set(t_sorted)
    gate_pad = jnp.zeros((p_len,), jnp.float32).at[dst].set(g_sorted)
    block_start = jnp.arange(n_blocks, dtype=jnp.int32) * MOE_BLOCK
    block_expert = jnp.minimum(jnp.searchsorted(padded_end, block_start, side='right'), N_EXPERTS - 1)
    x_pad = jnp.concatenate([xt, jnp.zeros((1, d), xt.dtype)], axis=0)[tok_pad].reshape(n_blocks, MOE_BLOCK, d)

    def expert_block(args):
        xb, e = args
        hid = xb @ w_gate_up[e] + b_gate_up[e]
        glu = jnp.minimum(hid[:, :D_FF], SWIGLU_LIMIT)
        lin = jnp.clip(hid[:, D_FF:], -SWIGLU_LIMIT, SWIGLU_LIMIT)
        act = glu * jax.nn.sigmoid(SWIGLU_ALPHA * glu) * (lin + 1.0)
        return act @ w_down[e] + b_down[e]

    y_pad = lax.map(expert_block, (x_pad, block_expert)).reshape(p_len, d)
    y = jnp.zeros((t + 1, d), jnp.float32).at[tok_pad].add(y_pad.astype(jnp.float32) * gate_pad[:, None])[:t]
    return y.astype(x.dtype).reshape(b, s, d)


def setup_inputs(seed: int = 0) -> dict:
    key = jax.random.key(seed)
    ks = jax.random.split(key, 24)
    f32 = jnp.float32
    nrm = lambda k, shape, scale: jax.random.normal(k, shape, f32) * scale
    dt = jnp.exp(jax.random.uniform(ks[5], (DEPTH, GDN_HEADS), f32, np.log(1e-3), np.log(1e-1)))
    return {
        'x': nrm(ks[0], (BATCH, SEQ, D_MODEL), 1.0),
        'mem': nrm(ks[1], (BATCH, MEM_LEN, D_MODEL), 1.0),
        'w_in': nrm(ks[2], (DEPTH, D_MODEL, D_IN), D_MODEL ** -0.5),
        'w_conv': nrm(ks[3], (DEPTH, GDN_CONV, GDN_CONV_CH), GDN_CONV ** -0.5),
        'a_log': jnp.log(jax.random.uniform(ks[4], (DEPTH, GDN_HEADS), f32, 1.0, 16.0)),
        'dt_bias': dt + jnp.log(-jnp.expm1(-dt)),
        'gdn_norm_w': 1.0 + nrm(ks[6], (DEPTH, GDN_DV), 0.01),
        'w_mem_kv': nrm(ks[7], (DEPTH, D_MODEL, 2 * MEM_W), D_MODEL ** -0.5),
        'w_p_gdn': nrm(ks[8], (DEPTH, GDN_V, D_MODEL), GDN_V ** -0.5),
        'w_p_sb': nrm(ks[9], (DEPTH, SB_W, D_MODEL), SB_W ** -0.5),
        'w_p_mem': nrm(ks[10], (DEPTH, MEM_W, D_MODEL), MEM_W ** -0.5),
        'w_o': nrm(ks[11], (DEPTH, D_MODEL, D_MODEL), D_MODEL ** -0.5 * DEEPNORM_BETA),
        'ln1_g': 1.0 + nrm(ks[12], (DEPTH, D_MODEL), 0.01),
        'ln1_b': nrm(ks[13], (DEPTH, D_MODEL), 0.01),
        'w_router': nrm(ks[14], (DEPTH, D_MODEL, N_EXPERTS), D_MODEL ** -0.5),
        'b_router': nrm(ks[15], (DEPTH, N_EXPERTS), 0.01),
        'w_gate_up': nrm(ks[16], (DEPTH, N_EXPERTS, D_MODEL, 2 * D_FF), D_MODEL ** -0.5),
        'b_gate_up': nrm(ks[17], (DEPTH, N_EXPERTS, 2 * D_FF), 0.01),
        'w_down': nrm(ks[18], (DEPTH, N_EXPERTS, D_FF, D_MODEL), D_FF ** -0.5 * DEEPNORM_BETA),
        'b_down': nrm(ks[19], (DEPTH, N_EXPERTS, D_MODEL), 0.01),
        'ln2_g': 1.0 + nrm(ks[20], (DEPTH, D_MODEL), 0.01),
        'ln2_b': nrm(ks[21], (DEPTH, D_MODEL), 0.01),
    }


def reference(x, mem, w_in, w_conv, a_log, dt_bias, gdn_norm_w, w_mem_kv, w_p_gdn, w_p_sb, w_p_mem, w_o,
              ln1_g, ln1_b, w_router, b_router, w_gate_up, b_gate_up, w_down, b_down, ln2_g, ln2_b):
    b, s, d = x.shape
    split_idx = np.cumsum(IN_SIZES)[:-1].tolist()
    for l in range(DEPTH):
        proj = x @ w_in[l]
        gq, gk, gv, gz, ga, gb, sq, sk, sv, mq, gates = jnp.split(proj, split_idx, axis=-1)

        qkv = jax.nn.silu(_causal_depthwise_conv(jnp.concatenate([gq, gk, gv], axis=-1), w_conv[l]))
        cq, ck, cv = jnp.split(qkv, [GDN_QK, 2 * GDN_QK], axis=-1)
        q = _l2norm(_heads(cq, GDN_HEADS).astype(jnp.float32)) * (GDN_DK ** -0.5)
        k = _l2norm(_heads(ck, GDN_HEADS).astype(jnp.float32))
        v = _heads(cv, GDN_HEADS).astype(jnp.float32)
        log_decay = -jnp.exp(a_log[l]) * jax.nn.softplus(ga.astype(jnp.float32) + dt_bias[l])
        beta = jax.nn.sigmoid(gb.astype(jnp.float32))
        o = _gated_delta_rule(q, k, v, log_decay.transpose(0, 2, 1), beta.transpose(0, 2, 1))
        o = o * lax.rsqrt(jnp.mean(o * o, -1, keepdims=True) + NORM_EPS) * gdn_norm_w[l].astype(jnp.float32)
        z = gz.reshape(b, s, GDN_HEADS, GDN_DV).astype(jnp.float32)
        y_gdn = (o.transpose(0, 2, 1, 3) * jax.nn.silu(z)).reshape(b, s, GDN_V).astype(x.dtype)

        y_sb = _stick_breaking_attention(_heads(sq, SB_HEADS), _heads(sk, SB_HEADS), _heads(sv, SB_HEADS))
        y_sb = y_sb.transpose(0, 2, 1, 3).reshape(b, s, SB_W)

        mk, mv = jnp.split(mem @ w_mem_kv[l], 2, axis=-1)
        y_mem = _memory_attention(_heads(mq, MEM_HEADS), _heads(mk, MEM_HEADS), _heads(mv, MEM_HEADS))
        y_mem = y_mem.transpose(0, 2, 1, 3).reshape(b, s, MEM_W)

        g = jax.nn.sigmoid(gates.reshape(b, s, N_BRANCHES, d))
        mixed = (g[:, :, 0] * (y_gdn @ w_p_gdn[l]) + g[:, :, 1] * (y_sb @ w_p_sb[l])
                 + g[:, :, 2] * (y_mem @ w_p_mem[l]))
        x = _layer_norm(DEEPNORM_ALPHA * x + mixed @ w_o[l], ln1_g[l], ln1_b[l])

        y_moe = _moe(x, w_router[l], b_router[l], w_gate_up[l], b_gate_up[l], w_down[l], b_down[l])
        x = _layer_norm(DEEPNORM_ALPHA * x + y_moe, ln2_g[l], ln2_b[l])
    return x
```

```python
import functools

import jax
import jax.numpy as jnp
from jax import lax
from jax.experimental import pallas as pl
from jax.experimental.pallas import tpu as pltpu

F32 = jnp.float32
BF16 = jnp.bfloat16

GDN_HEADS = 4
GDN_DK = 128
GDN_CONV = 4
GDN_CHUNK = 64
SB_HEADS = 8
SB_DH = 64
MEM_HEADS = 4
MEM_DH = 128
N_BRANCHES = 3
N_EXPERTS = 32
TOP_K = 4
SWIGLU_LIMIT = 7.0
SWIGLU_ALPHA = 1.702
LN_EPS = 1e-5
NORM_EPS = 1e-6
DEPTH = 1
DEEPNORM_ALPHA = (2 * DEPTH) ** 0.25

LANES = 128
MOE_ROWS = 256
VMEM_LIMIT = 56 * 1024 * 1024

NT = (((1,), (1,)), ((), ()))
TN = (((0,), (0,)), ((), ()))


def _cparams(sem):
    return pltpu.CompilerParams(dimension_semantics=sem, vmem_limit_bytes=VMEM_LIMIT)


def _dot(a, b, dims=None, precision=None):
    if dims is None:
        return jnp.dot(a, b, preferred_element_type=F32, precision=precision)
    return lax.dot_general(a, b, dims, preferred_element_type=F32, precision=precision)


def _iota(shape, dim):
    return lax.broadcasted_iota(jnp.int32, shape, dim)


def _softplus(z):
    return jnp.maximum(z, 0.0) + jnp.log1p(jnp.exp(-jnp.abs(z)))


def _sigmoid(z):
    return 1.0 / (1.0 + jnp.exp(-z))


def _proj_kernel(x_ref, w_ref, wab_ref, wabt_ref, o_ref, oab_ref, oabt_ref, *, n_chunk):
    xb = x_ref[...].astype(BF16)
    n = w_ref.shape[1]
    for j in range(0, n, n_chunk):
        o_ref[:, j:j + n_chunk] = _dot(xb, w_ref[:, j:j + n_chunk]).astype(o_ref.dtype)
    oab_ref[...] = _dot(xb, wab_ref[...])
    oabt_ref[...] = _dot(wabt_ref[...], xb, NT)


def _proj(x2d, w_main, w_ab, w_abt, tm=256, n_chunk=1024):
    t, d = x2d.shape
    tm = min(tm, t)
    n = w_main.shape[1]
    return pl.pallas_call(
        functools.partial(_proj_kernel, n_chunk=n_chunk),
        grid=(t // tm,),
        in_specs=[
            pl.BlockSpec((tm, d), lambda i: (i, 0)),
            pl.BlockSpec((d, n), lambda i: (0, 0), pipeline_mode=pl.Buffered(1)),
            pl.BlockSpec((d, LANES), lambda i: (0, 0), pipeline_mode=pl.Buffered(1)),
            pl.BlockSpec((16, d), lambda i: (0, 0), pipeline_mode=pl.Buffered(1)),
        ],
        out_specs=[
            pl.BlockSpec((tm, n), lambda i: (i, 0)),
            pl.BlockSpec((tm, LANES), lambda i: (i, 0)),
            pl.BlockSpec((16, tm), lambda i: (0, i)),
        ],
        out_shape=[
            jax.ShapeDtypeStruct((t, n), BF16),
            jax.ShapeDtypeStruct((t, LANES), F32),
            jax.ShapeDtypeStruct((16, t), F32),
        ],
        compiler_params=_cparams(("arbitrary",)),
        name="proj",
    )(x2d, w_main, w_ab, w_abt)


def _mm_kernel(x_ref, w_ref, o_ref):
    o_ref[...] = _dot(x_ref[...].astype(BF16), w_ref[...]).astype(o_ref.dtype)


def _mm(x2d, w, out_dtype, tm=512):
    t, d = x2d.shape
    tm = min(tm, t)
    n = w.shape[1]
    return pl.pallas_call(
        _mm_kernel,
        grid=(t // tm,),
        in_specs=[pl.BlockSpec((tm, d), lambda i: (i, 0)),
                  pl.BlockSpec((d, n), lambda i: (0, 0), pipeline_mode=pl.Buffered(1))],
        out_specs=pl.BlockSpec((tm, n), lambda i: (i, 0)),
        out_shape=jax.ShapeDtypeStruct((t, n), out_dtype),
        compiler_params=_cparams(("arbitrary",)),
        name="memkv",
    )(x2d, w)


def _memattn_kernel(q_ref, kv_ref, o_ref):
    w = MEM_HEADS * MEM_DH
    scale = MEM_DH ** -0.5
    for h in range(MEM_HEADS):
        q = q_ref[:, h * MEM_DH:(h + 1) * MEM_DH]
        k = kv_ref[:, h * MEM_DH:(h + 1) * MEM_DH]
        v = kv_ref[:, w + h * MEM_DH:w + (h + 1) * MEM_DH]
        s = _dot(q, k, NT) * scale
        p = jnp.exp(s - jnp.max(s, axis=1, keepdims=True))
        l = jnp.sum(p, axis=1, keepdims=True)
        o = _dot(p.astype(BF16), v) / l
        o_ref[:, h * MEM_DH:(h + 1) * MEM_DH] = o.astype(o_ref.dtype)


def _memattn(proj3d, memkv3d, q_col_block, tq=512):
    b, s, _ = proj3d.shape
    tq = min(tq, s)
    m = memkv3d.shape[1]
    w = MEM_HEADS * MEM_DH
    return pl.pallas_call(
        _memattn_kernel,
        grid=(b, s // tq),
        in_specs=[pl.BlockSpec((None, tq, w), lambda i, j: (i, j, q_col_block)),
                  pl.BlockSpec((None, m, 2 * w), lambda i, j: (i, 0, 0))],
        out_specs=pl.BlockSpec((None, tq, w), lambda i, j: (i, j, 0)),
        out_shape=jax.ShapeDtypeStruct((b, s, w), BF16),
        compiler_params=_cparams(("arbitrary", "arbitrary")),
        name="memattn",
    )(proj3d, memkv3d)


def _sb_kernel(q_ref, k_ref, v_ref, o_ref, *, blk):
    qi = pl.program_id(2)
    q = q_ref[...]
    lane = _iota((blk, LANES), 1)
    row = _iota((blk, blk), 0)
    col = _iota((blk, blk), 1)
    upper = (row > col).astype(BF16)
    causal = col < row
    scale = SB_DH ** -0.5

    def block(j, carry, acc, qm, diag):
        r0 = pl.multiple_of(j * blk, blk)
        kblk = k_ref[pl.ds(r0, blk), :]
        vblk = v_ref[pl.ds(r0, blk), :]
        z = _dot(qm, kblk, NT)
        sp = _softplus(z)
        ls = -sp
        if diag:
            ls = jnp.where(causal, ls, 0.0)
        hi = ls.astype(BF16)
        lo = (ls - hi.astype(F32)).astype(BF16)
        after = _dot(hi, upper) + _dot(lo, upper)
        w = jnp.exp((z - sp) + after + carry)
        if diag:
            w = jnp.where(causal, w, 0.0)
        acc = acc + _dot(w.astype(BF16), vblk)
        carry = carry + jnp.sum(ls, axis=1, keepdims=True)
        return carry, acc

    out = jnp.zeros((blk, LANES), F32)
    for hh in range(2):
        in_head = (lane >= hh * SB_DH) & (lane < (hh + 1) * SB_DH)
        qm = jnp.where(in_head, q * scale, 0.0).astype(BF16)
        carry, acc = block(qi, jnp.zeros((blk, 1), F32), jnp.zeros((blk, LANES), F32), qm, True)

        def body(i, c, qm=qm):
            return block(qi - 1 - i, c[0], c[1], qm, False)

        carry, acc = lax.fori_loop(0, qi, body, (carry, acc))
        out = jnp.where(in_head, acc, out)
    o_ref[...] = out.astype(o_ref.dtype)


def _sb(proj3d, q_cb, k_cb, v_cb, blk=128):
    b, s, _ = proj3d.shape
    pairs = SB_HEADS * SB_DH // LANES
    return pl.pallas_call(
        functools.partial(_sb_kernel, blk=blk),
        grid=(b, pairs, s // blk),
        in_specs=[pl.BlockSpec((None, blk, LANES), lambda i, p, j: (i, j, q_cb + p)),
                  pl.BlockSpec((None, s, LANES), lambda i, p, j: (i, 0, k_cb + p)),
                  pl.BlockSpec((None, s, LANES), lambda i, p, j: (i, 0, v_cb + p))],
        out_specs=pl.BlockSpec((None, blk, LANES), lambda i, p, j: (i, j, p)),
        out_shape=jax.ShapeDtypeStruct((b, s, pairs * LANES), BF16),
        compiler_params=_cparams(("arbitrary", "arbitrary", "arbitrary")),
        name="stickbreak",
    )(proj3d, proj3d, proj3d)


def _gdn_kernel(alog_ref, dtb_ref, q_ref, k_ref, v_ref, z_ref, gab_ref, gabt_ref,
                wq_ref, wk_ref, wv_ref, nw_ref, y_ref,
                qc, kc, vc, gcol, bcol, grow, u_s, w_s, qg_s, kd_s, at_s, gl_s, o_s):
    h = pl.program_id(1)
    s = q_ref.shape[0]
    c = GDN_CHUNK
    r = LANES
    n_tiles = s // r
    hi_p = lax.Precision.HIGHEST

    def conv_silu(x_ref, w_ref):
        x = x_ref[...].astype(F32)
        rowi = _iota(x.shape, 0)
        y = x * w_ref[GDN_CONV - 1:GDN_CONV, :]
        for d in range(1, GDN_CONV):
            xs = jnp.where(rowi >= d, pltpu.roll(x, d, 0), 0.0)
            y = y + xs * w_ref[GDN_CONV - 1 - d:GDN_CONV - d, :]
        return y * _sigmoid(y)

    def l2norm(t):
        return t * lax.rsqrt(jnp.sum(t * t, axis=1, keepdims=True) + NORM_EPS)

    qc[...] = l2norm(conv_silu(q_ref, wq_ref)) * (GDN_DK ** -0.5)
    kc[...] = l2norm(conv_silu(k_ref, wk_ref))
    vc[...] = conv_silu(v_ref, wv_ref)

    a_neg = -jnp.exp(jnp.full((1, 1), alog_ref[h], F32))
    dtb = dtb_ref[h]
    lane = _iota((s, LANES), 1)
    gab = gab_ref[...]
    ga = jnp.sum(jnp.where(lane == h, gab, 0.0), axis=1, keepdims=True)
    gb = jnp.sum(jnp.where(lane == h + GDN_HEADS, gab, 0.0), axis=1, keepdims=True)
    gcol[...] = jnp.broadcast_to(a_neg * _softplus(ga + dtb), (s, LANES))
    bcol[...] = jnp.broadcast_to(_sigmoid(gb), (s, LANES))
    g_row = a_neg * _softplus(gabt_ref[pl.ds(h, 1), :] + dtb)
    for i in range(n_tiles):
        grow[i] = jnp.broadcast_to(g_row[:, i * r:(i + 1) * r], (8, r))

    row = _iota((r, r), 0)
    col = _iota((r, r), 1)
    same = (row // c) == (col // c)
    m_incl = same & (col <= row)
    m_strict = same & (col < row)
    l_incl = m_incl.astype(F32)
    l_incl_t = (same & (row <= col)).astype(F32)
    l_same = same.astype(F32)
    eye = (row == col).astype(F32)

    def prep(i, _):
        r0 = pl.multiple_of(i * r, r)
        rows = pl.ds(r0, r)
        q = qc[rows, :]
        k = kc[rows, :]
        v = vc[rows, :]
        g = gcol[rows, :]
        beta = bcol[rows, :]
        gc = _dot(l_incl, g, precision=hi_p)
        gc_row = _dot(grow[i], l_incl_t, precision=hi_p)[0:1, :]
        glast = _dot(l_same, g, precision=hi_p)
        decay = jnp.where(m_incl, jnp.exp(jnp.where(m_incl, gc - gc_row, 0.0)), 0.0)
        kb = k.astype(BF16)
        kk = _dot(kb, kb, NT)
        n_mat = -jnp.where(m_strict, kk * decay * beta, 0.0)
        t_off = n_mat
        m_pow = n_mat
        for _unused in range(5):
            mb = m_pow.astype(BF16)
            m_pow = _dot(mb, mb)
            t_off = t_off + m_pow + _dot(m_pow.astype(BF16), t_off.astype(BF16))
        eg = jnp.exp(gc)
        rhs = jnp.concatenate([v * beta, k * (beta * eg)], axis=1)
        uw = rhs + _dot(t_off.astype(BF16), rhs.astype(BF16))
        attn = _dot(q.astype(BF16), kb, NT) * decay
        u_s[rows, :] = uw[:, :LANES]
        w_s[rows, :] = uw[:, LANES:].astype(BF16)
        qg_s[rows, :] = (q * eg).astype(BF16)
        kd_s[rows, :] = (k * jnp.exp(glast - gc)).astype(BF16)
        at_s[pl.ds(r0, c), :] = attn[0:c, 0:c].astype(BF16)
        at_s[pl.ds(r0 + c, c), :] = attn[c:2 * c, c:2 * c].astype(BF16)
        gl_s[rows, :] = glast
        return 0

    lax.fori_loop(0, n_tiles, prep, 0)

    def chunk(i, state):
        r0 = pl.multiple_of(i * c, c)
        rows = pl.ds(r0, c)
        sb = state.astype(BF16)
        v_new = u_s[rows, :] - _dot(w_s[rows, :], sb)
        vb = v_new.astype(BF16)
        o_s[rows, :] = _dot(qg_s[rows, :], sb) + _dot(at_s[rows, :], vb)
        return state * jnp.exp(gl_s[pl.ds(r0, 1), :]) + _dot(kd_s[rows, :], vb, TN)

    lax.fori_loop(0, s // c, chunk, jnp.zeros((GDN_DK, LANES), F32))

    o = o_s[...]
    zg = z_ref[...].astype(F32)
    o = o * lax.rsqrt(jnp.mean(o * o, axis=1, keepdims=True) + NORM_EPS) * nw_ref[...]
    y_ref[...] = (o * (zg * _sigmoid(zg))).astype(y_ref.dtype)


def _gdn(a_log, dt_bias, proj3d, gab3d, gabt, w_conv, norm_w, q_cb, k_cb, v_cb, z_cb):
    b, s, _ = proj3d.shape
    hh = GDN_HEADS
    smem = pl.BlockSpec(memory_space=pltpu.SMEM)

    def col(cb):
        return pl.BlockSpec((None, s, LANES), lambda i, h: (i, 0, cb + h))

    def wcol(cb):
        return pl.BlockSpec((GDN_CONV, LANES), lambda i, h: (0, cb + h))

    return pl.pallas_call(
        _gdn_kernel,
        grid=(b, hh),
        in_specs=[smem, smem, col(q_cb), col(k_cb), col(v_cb), col(z_cb),
                  pl.BlockSpec((None, s, LANES), lambda i, h: (i, 0, 0)),
                  pl.BlockSpec((16, s), lambda i, h: (0, i)),
                  wcol(0), wcol(hh), wcol(2 * hh),
                  pl.BlockSpec((1, LANES), lambda i, h: (0, 0))],
        out_specs=pl.BlockSpec((None, s, LANES), lambda i, h: (i, 0, h)),
        out_shape=jax.ShapeDtypeStruct((b, s, hh * LANES), BF16),
        scratch_shapes=[
            pltpu.VMEM((s, LANES), F32), pltpu.VMEM((s, LANES), F32), pltpu.VMEM((s, LANES), F32),
            pltpu.VMEM((s, LANES), F32), pltpu.VMEM((s, LANES), F32),
            pltpu.VMEM((s // LANES, 8, LANES), F32),
            pltpu.VMEM((s, LANES), F32), pltpu.VMEM((s, LANES), BF16),
            pltpu.VMEM((s, LANES), BF16), pltpu.VMEM((s, LANES), BF16),
            pltpu.VMEM((s, GDN_CHUNK), BF16),
            pltpu.VMEM((s, LANES), F32), pltpu.VMEM((s, LANES), F32),
        ],
        compiler_params=_cparams(("arbitrary", "arbitrary")),
        name="gdn",
    )(a_log, dt_bias, proj3d, proj3d, proj3d, proj3d, gab3d, gabt, w_conv, w_conv, w_conv, norm_w)


def _store_slabs(ref, val):
    n, d = val.shape
    nb = d // LANES
    for j in range(nb):
        ref[pl.ds(j, n, stride=nb), :] = val[:, j * LANES:(j + 1) * LANES]


def _load_slab_block(ref, j, n, nb):
    return ref[pl.ds(j, n, stride=nb), :]


def _slab_rows(t, nb):
    return pl.ds(pl.multiple_of(t * nb, nb), nb)


def _layer_norm(r, g, b):
    mu = jnp.mean(r, axis=1, keepdims=True)
    d = r - mu
    var = jnp.mean(d * d, axis=1, keepdims=True)
    return d * lax.rsqrt(var + LN_EPS) * g + b


def _merge_kernel(x_ref, yg_ref, ys_ref, ym_ref, g0_ref, g1_ref, g2_ref, wg_ref, ws_ref, wm_ref,
                  wo_ref, lg_ref, lb_ref, wr_ref, br_ref, x1_ref, route_ref, cnt_ref, carry):
    i = pl.program_id(0)
    tm = x_ref.shape[0]

    @pl.when(i == 0)
    def _():
        carry[...] = jnp.zeros_like(carry)

    def branch(y_ref, w_ref, g_ref):
        return _sigmoid(g_ref[...].astype(F32)) * _dot(y_ref[...], w_ref[...])

    mixed = branch(yg_ref, wg_ref, g0_ref) + branch(ys_ref, ws_ref, g1_ref) + branch(ym_ref, wm_ref, g2_ref)
    r = DEEPNORM_ALPHA * x_ref[...] + _dot(mixed.astype(BF16), wo_ref[...])
    x1 = _layer_norm(r, lg_ref[...], lb_ref[...])
    _store_slabs(x1_ref, x1)

    lane = _iota((tm, LANES), 1)
    lanef = lane.astype(F32)
    neg = jnp.float32(-jnp.inf)
    logits = jnp.where(lane < N_EXPERTS, _dot(x1.astype(BF16), wr_ref[...]) + br_ref[...], neg)
    vals, idxs = [], []
    for _unused in range(TOP_K):
        m = jnp.max(logits, axis=1, keepdims=True)
        idx = jnp.min(jnp.where(logits == m, lanef, float(LANES)), axis=1, keepdims=True)
        vals.append(m)
        idxs.append(idx)
        logits = jnp.where(lanef == idx, neg, logits)
    es = [jnp.exp(v - vals[0]) for v in vals]
    den = es[0] + es[1] + es[2] + es[3]
    onehot = jnp.zeros((tm, LANES), F32)
    for idx in idxs:
        onehot = onehot + (lanef == idx).astype(F32)
    row = _iota((tm, tm), 0)
    col = _iota((tm, tm), 1)
    before = _dot((col < row).astype(BF16), onehot.astype(BF16)) + carry[0:1, :]
    route = jnp.zeros((tm, LANES), F32)
    for k in range(TOP_K):
        rank = jnp.sum(jnp.where(lanef == idxs[k], before, 0.0), axis=1, keepdims=True)
        route = jnp.where(lane == k, idxs[k], route)
        route = jnp.where(lane == TOP_K + k, rank, route)
        route = jnp.where(lane == 2 * TOP_K + k, es[k] / den, route)
    route_ref[...] = route
    carry[...] = carry[...] + jnp.sum(onehot, axis=0, keepdims=True)
    cnt_ref[...] = carry[...]


def _merge(x2d, yg, ys, ym, proj, gate_cb, wg, ws, wm, wo, lg, lb, wr, br, tm=256):
    t, d = x2d.shape
    tm = min(tm, t)
    wb = yg.shape[1]

    def rows(w):
        return pl.BlockSpec((tm, w), lambda i: (i, 0))

    def res(shape):
        return pl.BlockSpec(shape, lambda i: (0, 0), pipeline_mode=pl.Buffered(1))

    def gate(j):
        return pl.BlockSpec((tm, d), lambda i: (i, gate_cb + j))

    return pl.pallas_call(
        _merge_kernel,
        grid=(t // tm,),
        in_specs=[rows(d), rows(wb), rows(wb), rows(wb), gate(0), gate(1), gate(2),
                  res((wb, d)), res((wb, d)), res((wb, d)), res((d, d)),
                  res((1, d)), res((1, d)), res((d, LANES)), res((1, LANES))],
        out_specs=[pl.BlockSpec((tm * (d // LANES), LANES), lambda i: (i, 0)), rows(LANES),
                   pl.BlockSpec((8, LANES), lambda i: (0, 0))],
        out_shape=[jax.ShapeDtypeStruct((t * (d // LANES), LANES), F32),
                   jax.ShapeDtypeStruct((t, LANES), F32), jax.ShapeDtypeStruct((8, LANES), F32)],
        scratch_shapes=[pltpu.VMEM((8, LANES), F32)],
        compiler_params=_cparams(("arbitrary",)),
        name="merge_route",
    )(x2d, yg, ys, ym, proj, proj, proj, wg, ws, wm, wo, lg, lb, wr, br)


def _tables_kernel(cnt_ref, route_ref, dst_ref, blk_ref, meta_ref, *, n_blocks):
    lane8 = _iota((8, LANES), 1)
    cnt = jnp.where(lane8 < N_EXPERTS, cnt_ref[...], 0.0)
    nblk = jnp.floor((cnt + (MOE_ROWS - 1)) * (1.0 / MOE_ROWS))
    row = _iota((LANES, LANES), 0)
    col = _iota((LANES, LANES), 1)
    end_blk = _dot(nblk.astype(BF16), (row <= col).astype(BF16))
    start_row = (end_blk - nblk) * MOE_ROWS
    start1 = start_row[0:1, :]
    end1 = end_blk[0:1, :]

    @pl.when(pl.program_id(0) == 0)
    def _():
        bi = _iota((n_blocks, LANES), 0).astype(F32)
        lane = _iota((n_blocks, LANES), 1)
        passed = jnp.where((lane < N_EXPERTS) & (end1 <= bi), 1.0, 0.0)
        be = jnp.minimum(jnp.sum(passed, axis=1, keepdims=True), N_EXPERTS - 1.0)
        blk_ref[...] = jnp.broadcast_to(be, (n_blocks, LANES)).astype(jnp.int32)
        meta = jnp.where(lane8 == 0, jnp.max(end_blk, axis=1, keepdims=True), 0.0)
        meta_ref[...] = meta.astype(jnp.int32)

    route = route_ref[...]
    tm = route.shape[0]
    lane = _iota((tm, LANES), 1)
    lanef = lane.astype(F32)
    out = jnp.zeros((tm, LANES), F32)
    for k in range(TOP_K):
        idx = jnp.sum(jnp.where(lane == k, route, 0.0), axis=1, keepdims=True)
        rank = jnp.sum(jnp.where(lane == TOP_K + k, route, 0.0), axis=1, keepdims=True)
        base = jnp.sum(jnp.where(lanef == idx, start1, 0.0), axis=1, keepdims=True)
        out = jnp.where(lane == k, base + rank, out)
    dst_ref[...] = out.astype(jnp.int32)


def _tables(counts, route, n_blocks, tm=1024):
    t = route.shape[0]
    tm = min(tm, t)
    return pl.pallas_call(
        functools.partial(_tables_kernel, n_blocks=n_blocks),
        grid=(t // tm,),
        in_specs=[pl.BlockSpec((8, LANES), lambda i: (0, 0)),
                  pl.BlockSpec((tm, LANES), lambda i: (i, 0))],
        out_specs=[pl.BlockSpec((tm, LANES), lambda i: (i, 0)),
                   pl.BlockSpec((n_blocks, LANES), lambda i: (0, 0)),
                   pl.BlockSpec((8, LANES), lambda i: (0, 0))],
        out_shape=[jax.ShapeDtypeStruct((t, LANES), jnp.int32),
                   jax.ShapeDtypeStruct((n_blocks, LANES), jnp.int32),
                   jax.ShapeDtypeStruct((8, LANES), jnp.int32)],
        compiler_params=_cparams(("arbitrary",)),
        name="route_tables",
    )(counts, route)


def _dispatch_kernel(dst_ref, cnt_ref, x_ref, xs_ref, zbuf, sem, zsem, *, nb):
    tt = x_ref.shape[0] // nb

    @pl.when(pl.program_id(0) == 0)
    def _():
        zbuf[...] = jnp.zeros_like(zbuf)

        def fill_expert(e, start):
            cnt = cnt_ref[e]
            padded = ((cnt + (MOE_ROWS - 1)) // MOE_ROWS) * MOE_ROWS
            n_pad = padded - cnt
            bit = MOE_ROWS // 2
            while bit >= 1:
                pos = start + cnt + (n_pad & ~(2 * bit - 1))

                @pl.when((n_pad & bit) != 0)
                def _(pos=pos, bit=bit):
                    cp = pltpu.make_async_copy(zbuf.at[pl.ds(0, bit * nb)],
                                               xs_ref.at[pl.ds(pl.multiple_of(pos * nb, nb), bit * nb)], zsem)
                    cp.start()
                    cp.wait()

                bit //= 2
            return start + padded

        used = lax.fori_loop(0, N_EXPERTS, fill_expert, 0)
        zrows = zbuf.shape[0] // nb

        def fill_tail(i, _):
            pos = used + i * zrows
            cp = pltpu.make_async_copy(zbuf, xs_ref.at[pl.ds(pl.multiple_of(pos * nb, nb), zrows * nb)], zsem)
            cp.start()
            cp.wait()
            return 0

        lax.fori_loop(0, (xs_ref.shape[0] // nb - used) // zrows, fill_tail, 0)

    def issue(t, _):
        for k in range(TOP_K):
            d = dst_ref[TOP_K * t + k]
            pltpu.make_async_copy(x_ref.at[_slab_rows(t, nb)], xs_ref.at[_slab_rows(d, nb)], sem).start()
        return 0

    lax.fori_loop(0, tt, issue, 0)
    for k in range(TOP_K):
        pltpu.make_async_copy(x_ref, xs_ref.at[pl.ds(0, tt * nb)], sem).wait()


def _dispatch(dst_flat, counts_i32, x1_slab, t, n_rows_total, tt=256):
    nb = x1_slab.shape[0] // t
    tt = min(tt, t)
    return pl.pallas_call(
        functools.partial(_dispatch_kernel, nb=nb),
        grid=(t // tt,),
        in_specs=[pl.BlockSpec((tt * TOP_K,), lambda i: (i,), memory_space=pltpu.SMEM),
                  pl.BlockSpec(memory_space=pltpu.SMEM),
                  pl.BlockSpec((tt * nb, LANES), lambda i: (i, 0))],
        out_specs=pl.BlockSpec(memory_space=pl.ANY),
        out_shape=jax.ShapeDtypeStruct((n_rows_total * nb, LANES), F32),
        scratch_shapes=[pltpu.VMEM((MOE_ROWS // 2 * nb, LANES), F32), pltpu.SemaphoreType.DMA,
                        pltpu.SemaphoreType.DMA],
        compiler_params=_cparams(("arbitrary",)),
        name="dispatch",
    )(dst_flat, counts_i32, x1_slab)


def _moe_kernel(be_ref, nu_ref, xs_ref, wgu_ref, bgu_ref, wd_ref, bd_ref, y_ref, xb):
    dff, d = wd_ref.shape
    nb = d // LANES

    @pl.when(pl.program_id(0) < nu_ref[0])
    def _():
        for j in range(nb):
            xb[:, j * LANES:(j + 1) * LANES] = _load_slab_block(xs_ref, j, MOE_ROWS, nb).astype(BF16)
        hid = _dot(xb[...], wgu_ref[...]) + bgu_ref[...]
        glu = jnp.minimum(hid[:, :dff], SWIGLU_LIMIT)
        lin = jnp.clip(hid[:, dff:], -SWIGLU_LIMIT, SWIGLU_LIMIT)
        act = glu * _sigmoid(SWIGLU_ALPHA * glu) * (lin + 1.0)
        _store_slabs(y_ref, _dot(act.astype(BF16), wd_ref[...]) + bd_ref[...])

    @pl.when(pl.program_id(0) >= nu_ref[0])
    def _():
        y_ref[...] = jnp.zeros_like(y_ref)


def _moe(block_expert, n_used, xs_slab2d, wgu, bgu, wd, bd):
    dff, d = wd.shape[1:]
    nb = d // LANES
    p = xs_slab2d.shape[0] // nb
    n_blocks = p // MOE_ROWS

    def rows(i, be, nu):
        return (jnp.minimum(i, nu[0] - 1), 0)

    def per_expert(i, be, nu):
        return (be[i], 0, 0)

    grid_spec = pltpu.PrefetchScalarGridSpec(
        num_scalar_prefetch=2,
        grid=(n_blocks,),
        in_specs=[pl.BlockSpec((MOE_ROWS * nb, LANES), rows),
                  pl.BlockSpec((None, d, 2 * dff), per_expert),
                  pl.BlockSpec((None, 1, 2 * dff), per_expert),
                  pl.BlockSpec((None, dff, d), per_expert),
                  pl.BlockSpec((None, 1, d), per_expert)],
        out_specs=pl.BlockSpec((MOE_ROWS * nb, LANES), lambda i, be, nu: (i, 0)),
        scratch_shapes=[pltpu.VMEM((MOE_ROWS, d), BF16)],
    )
    return pl.pallas_call(
        _moe_kernel,
        grid_spec=grid_spec,
        out_shape=jax.ShapeDtypeStruct((p * nb, LANES), F32),
        compiler_params=_cparams(("arbitrary",)),
        name="experts",
    )(block_expert, n_used, xs_slab2d, wgu, bgu, wd, bd)


def _combine_kernel(dst_ref, x1_ref, route_ref, lg_ref, lb_ref, y_ref, o_ref, buf, sem):
    tt, d = o_ref.shape
    nb = d // LANES

    def issue(t, _):
        for k in range(TOP_K):
            dd = dst_ref[TOP_K * t + k]
            pltpu.make_async_copy(y_ref.at[_slab_rows(dd, nb)], buf.at[k, _slab_rows(t, nb)], sem).start()
        return 0

    lax.fori_loop(0, tt, issue, 0)
    for k in range(TOP_K):
        pltpu.make_async_copy(y_ref.at[pl.ds(0, tt * nb)], buf.at[k], sem).wait()
    route = route_ref[...]
    lane = _iota(route.shape, 1)
    gates = [jnp.sum(jnp.where(lane == 2 * TOP_K + k, route, 0.0), axis=1, keepdims=True)
             for k in range(TOP_K)]
    blocks = []
    for j in range(nb):
        r = DEEPNORM_ALPHA * _load_slab_block(x1_ref, j, tt, nb)
        for k in range(TOP_K):
            r = r + gates[k] * _load_slab_block(buf.at[k], j, tt, nb)
        blocks.append(r)
    mu = sum(jnp.sum(r, axis=1, keepdims=True) for r in blocks) / d
    var = sum(jnp.sum((r - mu) * (r - mu), axis=1, keepdims=True) for r in blocks) / d
    inv = lax.rsqrt(var + LN_EPS)
    for j in range(nb):
        cols = slice(j * LANES, (j + 1) * LANES)
        o_ref[:, cols] = (blocks[j] - mu) * inv * lg_ref[:, cols] + lb_ref[:, cols]


def _combine(dst_flat, x1_slab, route, lg, lb, y_slab, tt=256):
    t = route.shape[0]
    tt = min(tt, t)
    nb = x1_slab.shape[0] // t
    d = nb * LANES
    return pl.pallas_call(
        _combine_kernel,
        grid=(t // tt,),
        in_specs=[pl.BlockSpec((tt * TOP_K,), lambda i: (i,), memory_space=pltpu.SMEM),
                  pl.BlockSpec((tt * nb, LANES), lambda i: (i, 0)),
                  pl.BlockSpec((tt, LANES), lambda i: (i, 0)),
                  pl.BlockSpec((1, d), lambda i: (0, 0)),
                  pl.BlockSpec((1, d), lambda i: (0, 0)),
                  pl.BlockSpec(memory_space=pl.ANY)],
        out_specs=pl.BlockSpec((tt, d), lambda i: (i, 0)),
        out_shape=jax.ShapeDtypeStruct((t, d), F32),
        scratch_shapes=[pltpu.VMEM((TOP_K, tt * nb, LANES), F32), pltpu.SemaphoreType.DMA],
        compiler_params=_cparams(("arbitrary",)),
        name="combine",
    )(dst_flat, x1_slab, route, lg, lb, y_slab)


def _pad_lanes(a, n=LANES):
    return jnp.pad(a, ((0, 0), (0, n - a.shape[1])))


def kernel(x, mem, w_in, w_conv, a_log, dt_bias, gdn_norm_w, w_mem_kv, w_p_gdn, w_p_sb, w_p_mem, w_o, ln1_g, ln1_b, w_router, b_router, w_gate_up, b_gate_up, w_down, b_down, ln2_g, ln2_b):
    b, s, d = x.shape
    t = b * s
    l = 0
    gdn_w = GDN_HEADS * GDN_DK
    sb_w = SB_HEADS * SB_DH
    mem_w = MEM_HEADS * MEM_DH
    ab0 = 4 * gdn_w
    ab1 = ab0 + 2 * GDN_HEADS
    w = w_in[l]
    w_main = jnp.concatenate([w[:, :ab0], w[:, ab1:]], axis=1).astype(BF16)
    w_ab = w[:, ab0:ab1].astype(BF16)
    x2d = x.reshape(t, d)
    proj, gab, gabt = _proj(x2d, w_main, _pad_lanes(w_ab), jnp.pad(w_ab.T, ((0, 16 - 2 * GDN_HEADS), (0, 0))))
    proj3d = proj.reshape(b, s, -1)
    cb_gq, cb_gk, cb_gv, cb_gz = 0, gdn_w // LANES, 2 * gdn_w // LANES, 3 * gdn_w // LANES
    cb_sq = ab0 // LANES
    cb_sk = cb_sq + sb_w // LANES
    cb_sv = cb_sk + sb_w // LANES
    off_mq = ab0 + 3 * sb_w
    off_gates = off_mq + mem_w

    y_gdn = _gdn(a_log[l], dt_bias[l], proj3d, gab.reshape(b, s, LANES), gabt, w_conv[l],
                 gdn_norm_w[l].reshape(1, -1), cb_gq, cb_gk, cb_gv, cb_gz)
    y_sb = _sb(proj3d, cb_sq, cb_sk, cb_sv)
    memkv = _mm(mem.reshape(-1, d), w_mem_kv[l].astype(BF16), BF16)
    y_mem = _memattn(proj3d, memkv.reshape(b, mem.shape[1], -1), off_mq // mem_w)

    x1, route, counts = _merge(
        x2d, y_gdn.reshape(t, -1), y_sb.reshape(t, -1), y_mem.reshape(t, -1), proj, off_gates // d,
        w_p_gdn[l].astype(BF16), w_p_sb[l].astype(BF16), w_p_mem[l].astype(BF16), w_o[l].astype(BF16),
        ln1_g[l].reshape(1, d), ln1_b[l].reshape(1, d),
        _pad_lanes(w_router[l]).astype(BF16), _pad_lanes(b_router[l].reshape(1, -1)))

    n_blocks = -(-(t * TOP_K) // MOE_ROWS) + N_EXPERTS
    n_blocks_pad = -(-n_blocks // 8) * 8
    dst, blk, meta = _tables(counts, route, n_blocks_pad)
    dst_flat = dst[:, :TOP_K].reshape(-1)
    xs = _dispatch(dst_flat, counts[0, :N_EXPERTS].astype(jnp.int32), x1, t, n_blocks * MOE_ROWS)
    y_pad = _moe(blk[:n_blocks, 0], meta[0, :1], xs,
                 w_gate_up[l].astype(BF16), b_gate_up[l][:, None, :],
                 w_down[l].astype(BF16), b_down[l][:, None, :])
    out = _combine(dst_flat, x1, route, ln2_g[l].reshape(1, d), ln2_b[l].reshape(1, d), y_pad)
    return out.reshape(b, s, d)
```

```python
import functools

import jax
import jax.numpy as jnp
from jax import lax
from jax.experimental import pallas as pl
from jax.experimental.pallas import tpu as pltpu

F32 = jnp.float32
BF16 = jnp.bfloat16

GDN_HEADS = 4
GDN_DK = 128
GDN_CONV = 4
GDN_CHUNK = 64
SB_HEADS = 8
SB_DH = 64
MEM_HEADS = 4
MEM_DH = 128
N_BRANCHES = 3
N_EXPERTS = 32
TOP_K = 4
SWIGLU_LIMIT = 7.0
SWIGLU_ALPHA = 1.702
LN_EPS = 1e-5
NORM_EPS = 1e-6
DEPTH = 1
DEEPNORM_ALPHA = (2 * DEPTH) ** 0.25

LANES = 128
MOE_ROWS = 256
VMEM_LIMIT = 56 * 1024 * 1024

NT = (((1,), (1,)), ((), ()))
TN = (((0,), (0,)), ((), ()))


def _cparams(sem):
    return pltpu.CompilerParams(dimension_semantics=sem, vmem_limit_bytes=VMEM_LIMIT)


def _dot(a, b, dims=None, precision=None):
    if dims is None:
        return jnp.dot(a, b, preferred_element_type=F32, precision=precision)
    return lax.dot_general(a, b, dims, preferred_element_type=F32, precision=precision)


def _iota(shape, dim):
    return lax.broadcasted_iota(jnp.int32, shape, dim)


def _softplus(z):
    return jnp.maximum(z, 0.0) + jnp.log1p(jnp.exp(-jnp.abs(z)))


def _sigmoid(z):
    return 1.0 / (1.0 + jnp.exp(-z))


def _proj_kernel(x_ref, w_ref, wab_ref, wabt_ref, o_ref, oab_ref, oabt_ref, *, n_chunk):
    xb = x_ref[...].astype(BF16)
    n = w_ref.shape[1]
    for j in range(0, n, n_chunk):
        o_ref[:, j:j + n_chunk] = _dot(xb, w_ref[:, j:j + n_chunk]).astype(o_ref.dtype)
    oab_ref[...] = _dot(xb, wab_ref[...])
    oabt_ref[...] = _dot(wabt_ref[...], xb, NT)


def _proj(x2d, w_main, w_ab, w_abt, tm=256, n_chunk=1024):
    t, d = x2d.shape
    tm = min(tm, t)
    n = w_main.shape[1]
    return pl.pallas_call(
        functools.partial(_proj_kernel, n_chunk=n_chunk),
        grid=(t // tm,),
        in_specs=[
            pl.BlockSpec((tm, d), lambda i: (i, 0)),
            pl.BlockSpec((d, n), lambda i: (0, 0), pipeline_mode=pl.Buffered(1)),
            pl.BlockSpec((d, LANES), lambda i: (0, 0), pipeline_mode=pl.Buffered(1)),
            pl.BlockSpec((16, d), lambda i: (0, 0), pipeline_mode=pl.Buffered(1)),
        ],
        out_specs=[
            pl.BlockSpec((tm, n), lambda i: (i, 0)),
            pl.BlockSpec((tm, LANES), lambda i: (i, 0)),
            pl.BlockSpec((16, tm), lambda i: (0, i)),
        ],
        out_shape=[
            jax.ShapeDtypeStruct((t, n), BF16),
            jax.ShapeDtypeStruct((t, LANES), F32),
            jax.ShapeDtypeStruct((16, t), F32),
        ],
        compiler_params=_cparams(("arbitrary",)),
        name="proj",
    )(x2d, w_main, w_ab, w_abt)


def _mm_kernel(x_ref, w_ref, o_ref):
    o_ref[...] = _dot(x_ref[...].astype(BF16), w_ref[...]).astype(o_ref.dtype)


def _mm(x2d, w, out_dtype, tm=512):
    t, d = x2d.shape
    tm = min(tm, t)
    n = w.shape[1]
    return pl.pallas_call(
        _mm_kernel,
        grid=(t // tm,),
        in_specs=[pl.BlockSpec((tm, d), lambda i: (i, 0)),
                  pl.BlockSpec((d, n), lambda i: (0, 0), pipeline_mode=pl.Buffered(1))],
        out_specs=pl.BlockSpec((tm, n), lambda i: (i, 0)),
        out_shape=jax.ShapeDtypeStruct((t, n), out_dtype),
        compiler_params=_cparams(("arbitrary",)),
        name="memkv",
    )(x2d, w)


def _memattn_kernel(q_ref, kv_ref, o_ref):
    w = MEM_HEADS * MEM_DH
    scale = MEM_DH ** -0.5
    for h in range(MEM_HEADS):
        q = q_ref[:, h * MEM_DH:(h + 1) * MEM_DH]
        k = kv_ref[:, h * MEM_DH:(h + 1) * MEM_DH]
        v = kv_ref[:, w + h * MEM_DH:w + (h + 1) * MEM_DH]
        s = _dot(q, k, NT) * scale
        p = jnp.exp(s - jnp.max(s, axis=1, keepdims=True))
        l = jnp.sum(p, axis=1, keepdims=True)
        o = _dot(p.astype(BF16), v) / l
        o_ref[:, h * MEM_DH:(h + 1) * MEM_DH] = o.astype(o_ref.dtype)


def _memattn(proj3d, memkv3d, q_col_block, tq=512):
    b, s, _ = proj3d.shape
    tq = min(tq, s)
    m = memkv3d.shape[1]
    w = MEM_HEADS * MEM_DH
    return pl.pallas_call(
        _memattn_kernel,
        grid=(b, s // tq),
        in_specs=[pl.BlockSpec((None, tq, w), lambda i, j: (i, j, q_col_block)),
                  pl.BlockSpec((None, m, 2 * w), lambda i, j: (i, 0, 0))],
        out_specs=pl.BlockSpec((None, tq, w), lambda i, j: (i, j, 0)),
        out_shape=jax.ShapeDtypeStruct((b, s, w), BF16),
        compiler_params=_cparams(("arbitrary", "arbitrary")),
        name="memattn",
    )(proj3d, memkv3d)


def _sb_kernel(q_ref, k_ref, v_ref, o_ref, *, blk):
    qi = pl.program_id(2)
    lane = _iota((blk, LANES), 1)
    first = lane < SB_DH
    qs = (q_ref[...].astype(F32) * (SB_DH ** -0.5)).astype(BF16)
    zero = jnp.zeros_like(qs)
    q2 = jnp.concatenate([jnp.where(first, qs, zero), jnp.where(first, zero, qs)], axis=0)
    row = _iota((2 * blk, blk), 0)
    col = _iota((2 * blk, blk), 1)
    causal = col < jnp.where(row >= blk, row - blk, row)
    upper = (_iota((blk, blk), 0) > _iota((blk, blk), 1)).astype(BF16)

    def block(j, carry, acc, diag):
        r0 = pl.multiple_of(j * blk, blk)
        z = _dot(q2, k_ref[pl.ds(r0, blk), :], NT)
        sp = _softplus(z)
        if diag:
            sp = jnp.where(causal, sp, 0.0)
        after = _dot(sp.astype(BF16), upper)
        w = jnp.exp(((z - sp) - after) - carry)
        if diag:
            w = jnp.where(causal, w, 0.0)
        acc = acc + _dot(w.astype(BF16), v_ref[pl.ds(r0, blk), :])
        return carry + jnp.sum(sp, axis=1, keepdims=True), acc

    carry, acc = block(qi, jnp.zeros((2 * blk, 1), F32), jnp.zeros((2 * blk, LANES), F32), True)
    carry, acc = lax.fori_loop(0, qi, lambda i, c: block(qi - 1 - i, c[0], c[1], False), (carry, acc))
    o_ref[...] = jnp.where(first, acc[:blk], acc[blk:]).astype(o_ref.dtype)


def _sb(proj3d, q_cb, k_cb, v_cb, blk=256):
    b, s, _ = proj3d.shape
    pairs = SB_HEADS * SB_DH // LANES
    return pl.pallas_call(
        functools.partial(_sb_kernel, blk=blk),
        grid=(b, pairs, s // blk),
        in_specs=[pl.BlockSpec((None, blk, LANES), lambda i, p, j: (i, j, q_cb + p)),
                  pl.BlockSpec((None, s, LANES), lambda i, p, j: (i, 0, k_cb + p)),
                  pl.BlockSpec((None, s, LANES), lambda i, p, j: (i, 0, v_cb + p))],
        out_specs=pl.BlockSpec((None, blk, LANES), lambda i, p, j: (i, j, p)),
        out_shape=jax.ShapeDtypeStruct((b, s, pairs * LANES), BF16),
        compiler_params=_cparams(("arbitrary", "arbitrary", "arbitrary")),
        name="stickbreak",
    )(proj3d, proj3d, proj3d)


def _gdn_kernel(alog_ref, dtb_ref, q_ref, k_ref, v_ref, z_ref, gab_ref, gabt_ref,
                wq_ref, wk_ref, wv_ref, nw_ref, y_ref,
                qc, kc, vc, gcol, bcol, grow, u_s, w_s, qg_s, kd_s, at_s, gl_s, o_s):
    h = pl.program_id(1)
    s = q_ref.shape[0]
    c = GDN_CHUNK
    r = LANES
    n_tiles = s // r
    hi_p = lax.Precision.HIGHEST

    def conv_silu(x_ref, w_ref):
        x = x_ref[...].astype(F32)
        rowi = _iota(x.shape, 0)
        y = x * w_ref[GDN_CONV - 1:GDN_CONV, :]
        for d in range(1, GDN_CONV):
            xs = jnp.where(rowi >= d, pltpu.roll(x, d, 0), 0.0)
            y = y + xs * w_ref[GDN_CONV - 1 - d:GDN_CONV - d, :]
        return y * _sigmoid(y)

    def l2norm(t):
        return t * lax.rsqrt(jnp.sum(t * t, axis=1, keepdims=True) + NORM_EPS)

    qc[...] = l2norm(conv_silu(q_ref, wq_ref)) * (GDN_DK ** -0.5)
    kc[...] = l2norm(conv_silu(k_ref, wk_ref))
    vc[...] = conv_silu(v_ref, wv_ref)

    a_neg = -jnp.exp(jnp.full((1, 1), alog_ref[h], F32))
    dtb = dtb_ref[h]
    lane = _iota((s, LANES), 1)
    gab = gab_ref[...]
    ga = jnp.sum(jnp.where(lane == h, gab, 0.0), axis=1, keepdims=True)
    gb = jnp.sum(jnp.where(lane == h + GDN_HEADS, gab, 0.0), axis=1, keepdims=True)
    gcol[...] = jnp.broadcast_to(a_neg * _softplus(ga + dtb), (s, LANES))
    bcol[...] = jnp.broadcast_to(_sigmoid(gb), (s, LANES))
    g_row = a_neg * _softplus(gabt_ref[pl.ds(h, 1), :] + dtb)
    for i in range(n_tiles):
        grow[i] = jnp.broadcast_to(g_row[:, i * r:(i + 1) * r], (8, r))

    row = _iota((r, r), 0)
    col = _iota((r, r), 1)
    same = (row // c) == (col // c)
    m_incl = same & (col <= row)
    m_strict = same & (col < row)
    l_incl = m_incl.astype(F32)
    l_incl_t = (same & (row <= col)).astype(F32)
    l_same = same.astype(F32)

    def prep(i, _):
        r0 = pl.multiple_of(i * r, r)
        rows = pl.ds(r0, r)
        q = qc[rows, :]
        k = kc[rows, :]
        v = vc[rows, :]
        g = gcol[rows, :]
        beta = bcol[rows, :]
        gc = _dot(l_incl, g, precision=hi_p)
        gc_row = _dot(grow[i], l_incl_t, precision=hi_p)[0:1, :]
        glast = _dot(l_same, g, precision=hi_p)
        decay = jnp.where(m_incl, jnp.exp(jnp.where(m_incl, gc - gc_row, 0.0)), 0.0)
        kb = k.astype(BF16)
        kk = _dot(kb, kb, NT)
        n_mat = -jnp.where(m_strict, kk * decay * beta, 0.0)
        t_off = n_mat
        m_pow = n_mat
        for _unused in range(5):
            mb = m_pow.astype(BF16)
            m_pow = _dot(mb, mb)
            t_off = t_off + m_pow + _dot(m_pow.astype(BF16), t_off.astype(BF16))
        eg = jnp.exp(gc)
        rhs = jnp.concatenate([v * beta, k * (beta * eg)], axis=1)
        uw = rhs + _dot(t_off.astype(BF16), rhs.astype(BF16))
        attn = _dot(q.astype(BF16), kb, NT) * decay
        u_s[rows, :] = uw[:, :LANES]
        w_s[rows, :] = uw[:, LANES:].astype(BF16)
        qg_s[rows, :] = (q * eg).astype(BF16)
        kd_s[rows, :] = (k * jnp.exp(glast - gc)).astype(BF16)
        at_s[pl.ds(r0, c), :] = attn[0:c, 0:c].astype(BF16)
        at_s[pl.ds(r0 + c, c), :] = attn[c:2 * c, c:2 * c].astype(BF16)
        gl_s[rows, :] = glast
        return 0

    lax.fori_loop(0, n_tiles, prep, 0)

    def chunk(i, state):
        r0 = pl.multiple_of(i * c, c)
        rows = pl.ds(r0, c)
        sb = state.astype(BF16)
        v_new = u_s[rows, :] - _dot(w_s[rows, :], sb)
        vb = v_new.astype(BF16)
        o_s[rows, :] = _dot(qg_s[rows, :], sb) + _dot(at_s[rows, :], vb)
        return state * jnp.exp(gl_s[pl.ds(r0, 1), :]) + _dot(kd_s[rows, :], vb, TN)

    lax.fori_loop(0, s // c, chunk, jnp.zeros((GDN_DK, LANES), F32))

    o = o_s[...]
    zg = z_ref[...].astype(F32)
    o = o * lax.rsqrt(jnp.mean(o * o, axis=1, keepdims=True) + NORM_EPS) * nw_ref[...]
    y_ref[...] = (o * (zg * _sigmoid(zg))).astype(y_ref.dtype)


def _gdn(a_log, dt_bias, proj3d, gab3d, gabt, w_conv, norm_w, q_cb, k_cb, v_cb, z_cb):
    b, s, _ = proj3d.shape
    hh = GDN_HEADS
    smem = pl.BlockSpec(memory_space=pltpu.SMEM)

    def col(cb):
        return pl.BlockSpec((None, s, LANES), lambda i, h: (i, 0, cb + h))

    def wcol(cb):
        return pl.BlockSpec((GDN_CONV, LANES), lambda i, h: (0, cb + h))

    return pl.pallas_call(
        _gdn_kernel,
        grid=(b, hh),
        in_specs=[smem, smem, col(q_cb), col(k_cb), col(v_cb), col(z_cb),
                  pl.BlockSpec((None, s, LANES), lambda i, h: (i, 0, 0)),
                  pl.BlockSpec((16, s), lambda i, h: (0, i)),
                  wcol(0), wcol(hh), wcol(2 * hh),
                  pl.BlockSpec((1, LANES), lambda i, h: (0, 0))],
        out_specs=pl.BlockSpec((None, s, LANES), lambda i, h: (i, 0, h)),
        out_shape=jax.ShapeDtypeStruct((b, s, hh * LANES), BF16),
        scratch_shapes=[
            pltpu.VMEM((s, LANES), F32), pltpu.VMEM((s, LANES), F32), pltpu.VMEM((s, LANES), F32),
            pltpu.VMEM((s, LANES), F32), pltpu.VMEM((s, LANES), F32),
            pltpu.VMEM((s // LANES, 8, LANES), F32),
            pltpu.VMEM((s, LANES), F32), pltpu.VMEM((s, LANES), BF16),
            pltpu.VMEM((s, LANES), BF16), pltpu.VMEM((s, LANES), BF16),
            pltpu.VMEM((s, GDN_CHUNK), BF16),
            pltpu.VMEM((s, LANES), F32), pltpu.VMEM((s, LANES), F32),
        ],
        compiler_params=_cparams(("arbitrary", "arbitrary")),
        name="gdn",
    )(a_log, dt_bias, proj3d, proj3d, proj3d, proj3d, gab3d, gabt, w_conv, w_conv, w_conv, norm_w)


def _store_slabs(ref, val):
    n, d = val.shape
    nb = d // LANES
    for j in range(nb):
        ref[pl.ds(j, n, stride=nb), :] = val[:, j * LANES:(j + 1) * LANES]


def _load_slab_block(ref, j, n, nb):
    return ref[pl.ds(j, n, stride=nb), :]


def _slab_rows(t, nb):
    return pl.ds(pl.multiple_of(t * nb, nb), nb)


def _layer_norm(r, g, b):
    mu = jnp.mean(r, axis=1, keepdims=True)
    d = r - mu
    var = jnp.mean(d * d, axis=1, keepdims=True)
    return d * lax.rsqrt(var + LN_EPS) * g + b


def _merge_kernel(x_ref, yg_ref, ys_ref, ym_ref, g0_ref, g1_ref, g2_ref, wg_ref, ws_ref, wm_ref,
                  wo_ref, lg_ref, lb_ref, wr_ref, br_ref, x1_ref, route_ref, cnt_ref, carry):
    i = pl.program_id(0)
    tm = x_ref.shape[0]

    @pl.when(i == 0)
    def _():
        carry[...] = jnp.zeros_like(carry)

    def branch(y_ref, w_ref, g_ref):
        return _sigmoid(g_ref[...].astype(F32)) * _dot(y_ref[...], w_ref[...])

    mixed = branch(yg_ref, wg_ref, g0_ref) + branch(ys_ref, ws_ref, g1_ref) + branch(ym_ref, wm_ref, g2_ref)
    r = DEEPNORM_ALPHA * x_ref[...] + _dot(mixed.astype(BF16), wo_ref[...])
    x1 = _layer_norm(r, lg_ref[...], lb_ref[...])
    _store_slabs(x1_ref, x1)

    lane = _iota((tm, LANES), 1)
    lanef = lane.astype(F32)
    neg = jnp.float32(-jnp.inf)
    logits = jnp.where(lane < N_EXPERTS, _dot(x1.astype(BF16), wr_ref[...]) + br_ref[...], neg)
    vals, idxs = [], []
    for _unused in range(TOP_K):
        m = jnp.max(logits, axis=1, keepdims=True)
        idx = jnp.min(jnp.where(logits == m, lanef, float(LANES)), axis=1, keepdims=True)
        vals.append(m)
        idxs.append(idx)
        logits = jnp.where(lanef == idx, neg, logits)
    es = [jnp.exp(v - vals[0]) for v in vals]
    den = es[0] + es[1] + es[2] + es[3]
    onehot = jnp.zeros((tm, LANES), F32)
    for idx in idxs:
        onehot = onehot + (lanef == idx).astype(F32)
    row = _iota((tm, tm), 0)
    col = _iota((tm, tm), 1)
    before = _dot((col < row).astype(BF16), onehot.astype(BF16)) + carry[0:1, :]
    route = jnp.zeros((tm, LANES), F32)
    for k in range(TOP_K):
        rank = jnp.sum(jnp.where(lanef == idxs[k], before, 0.0), axis=1, keepdims=True)
        route = jnp.where(lane == k, idxs[k], route)
        route = jnp.where(lane == TOP_K + k, rank, route)
        route = jnp.where(lane == 2 * TOP_K + k, es[k] / den, route)
    route_ref[...] = route
    carry[...] = carry[...] + jnp.sum(onehot, axis=0, keepdims=True)
    cnt_ref[...] = carry[...]


def _merge(x2d, yg, ys, ym, proj, gate_cb, wg, ws, wm, wo, lg, lb, wr, br, tm=256):
    t, d = x2d.shape
    tm = min(tm, t)
    wb = yg.shape[1]

    def rows(w):
        return pl.BlockSpec((tm, w), lambda i: (i, 0))

    def res(shape):
        return pl.BlockSpec(shape, lambda i: (0, 0), pipeline_mode=pl.Buffered(1))

    def gate(j):
        return pl.BlockSpec((tm, d), lambda i: (i, gate_cb + j))

    return pl.pallas_call(
        _merge_kernel,
        grid=(t // tm,),
        in_specs=[rows(d), rows(wb), rows(wb), rows(wb), gate(0), gate(1), gate(2),
                  res((wb, d)), res((wb, d)), res((wb, d)), res((d, d)),
                  res((1, d)), res((1, d)), res((d, LANES)), res((1, LANES))],
        out_specs=[pl.BlockSpec((tm * (d // LANES), LANES), lambda i: (i, 0)), rows(LANES),
                   pl.BlockSpec((8, LANES), lambda i: (0, 0))],
        out_shape=[jax.ShapeDtypeStruct((t * (d // LANES), LANES), F32),
                   jax.ShapeDtypeStruct((t, LANES), F32), jax.ShapeDtypeStruct((8, LANES), F32)],
        scratch_shapes=[pltpu.VMEM((8, LANES), F32)],
        compiler_params=_cparams(("arbitrary",)),
        name="merge_route",
    )(x2d, yg, ys, ym, proj, proj, proj, wg, ws, wm, wo, lg, lb, wr, br)


def _tables_kernel(cnt_ref, route_ref, dst_ref, blk_ref, meta_ref, *, n_blocks):
    lane8 = _iota((8, LANES), 1)
    cnt = jnp.where(lane8 < N_EXPERTS, cnt_ref[...], 0.0)
    nblk = jnp.floor((cnt + (MOE_ROWS - 1)) * (1.0 / MOE_ROWS))
    row = _iota((LANES, LANES), 0)
    col = _iota((LANES, LANES), 1)
    end_blk = _dot(nblk.astype(BF16), (row <= col).astype(BF16))
    start_row = (end_blk - nblk) * MOE_ROWS
    start1 = start_row[0:1, :]
    end1 = end_blk[0:1, :]

    @pl.when(pl.program_id(0) == 0)
    def _():
        bi = _iota((n_blocks, LANES), 0).astype(F32)
        lane = _iota((n_blocks, LANES), 1)
        passed = jnp.where((lane < N_EXPERTS) & (end1 <= bi), 1.0, 0.0)
        be = jnp.minimum(jnp.sum(passed, axis=1, keepdims=True), N_EXPERTS - 1.0)
        blk_ref[...] = jnp.broadcast_to(be, (n_blocks, LANES)).astype(jnp.int32)
        meta = jnp.where(lane8 == 0, jnp.max(end_blk, axis=1, keepdims=True), 0.0)
        meta_ref[...] = meta.astype(jnp.int32)

    route = route_ref[...]
    tm = route.shape[0]
    lane = _iota((tm, LANES), 1)
    lanef = lane.astype(F32)
    out = jnp.zeros((tm, LANES), F32)
    for k in range(TOP_K):
        idx = jnp.sum(jnp.where(lane == k, route, 0.0), axis=1, keepdims=True)
        rank = jnp.sum(jnp.where(lane == TOP_K + k, route, 0.0), axis=1, keepdims=True)
        base = jnp.sum(jnp.where(lanef == idx, start1, 0.0), axis=1, keepdims=True)
        out = jnp.where(lane == k, base + rank, out)
    dst_ref[...] = out.astype(jnp.int32)


def _tables(counts, route, n_blocks, tm=1024):
    t = route.shape[0]
    tm = min(tm, t)
    return pl.pallas_call(
        functools.partial(_tables_kernel, n_blocks=n_blocks),
        grid=(t // tm,),
        in_specs=[pl.BlockSpec((8, LANES), lambda i: (0, 0)),
                  pl.BlockSpec((tm, LANES), lambda i: (i, 0))],
        out_specs=[pl.BlockSpec((tm, LANES), lambda i: (i, 0)),
                   pl.BlockSpec((n_blocks, LANES), lambda i: (0, 0)),
                   pl.BlockSpec((8, LANES), lambda i: (0, 0))],
        out_shape=[jax.ShapeDtypeStruct((t, LANES), jnp.int32),
                   jax.ShapeDtypeStruct((n_blocks, LANES), jnp.int32),
                   jax.ShapeDtypeStruct((8, LANES), jnp.int32)],
        compiler_params=_cparams(("arbitrary",)),
        name="route_tables",
    )(counts, route)


def _dispatch_kernel(dst_ref, cnt_ref, x_ref, xs_ref, zbuf, sem, zsem, *, nb):
    tt = x_ref.shape[0] // nb

    @pl.when(pl.program_id(0) == 0)
    def _():
        zbuf[...] = jnp.zeros_like(zbuf)

        def fill_expert(e, start):
            cnt = cnt_ref[e]
            padded = ((cnt + (MOE_ROWS - 1)) // MOE_ROWS) * MOE_ROWS
            n_pad = padded - cnt
            bit = MOE_ROWS // 2
            while bit >= 1:
                pos = start + cnt + (n_pad & ~(2 * bit - 1))

                @pl.when((n_pad & bit) != 0)
                def _(pos=pos, bit=bit):
                    cp = pltpu.make_async_copy(zbuf.at[pl.ds(0, bit * nb)],
                                               xs_ref.at[pl.ds(pl.multiple_of(pos * nb, nb), bit * nb)], zsem)
                    cp.start()
                    cp.wait()

                bit //= 2
            return start + padded

        used = lax.fori_loop(0, N_EXPERTS, fill_expert, 0)
        zrows = zbuf.shape[0] // nb

        def fill_tail(i, _):
            pos = used + i * zrows
            cp = pltpu.make_async_copy(zbuf, xs_ref.at[pl.ds(pl.multiple_of(pos * nb, nb), zrows * nb)], zsem)
            cp.start()
            cp.wait()
            return 0

        lax.fori_loop(0, (xs_ref.shape[0] // nb - used) // zrows, fill_tail, 0)

    def issue(t, _):
        for k in range(TOP_K):
            d = dst_ref[TOP_K * t + k]
            pltpu.make_async_copy(x_ref.at[_slab_rows(t, nb)], xs_ref.at[_slab_rows(d, nb)], sem).start()
        return 0

    lax.fori_loop(0, tt, issue, 0)
    for k in range(TOP_K):
        pltpu.make_async_copy(x_ref, xs_ref.at[pl.ds(0, tt * nb)], sem).wait()


def _dispatch(dst_flat, counts_i32, x1_slab, t, n_rows_total, tt=256):
    nb = x1_slab.shape[0] // t
    tt = min(tt, t)
    return pl.pallas_call(
        functools.partial(_dispatch_kernel, nb=nb),
        grid=(t // tt,),
        in_specs=[pl.BlockSpec((tt * TOP_K,), lambda i: (i,), memory_space=pltpu.SMEM),
                  pl.BlockSpec(memory_space=pltpu.SMEM),
                  pl.BlockSpec((tt * nb, LANES), lambda i: (i, 0))],
        out_specs=pl.BlockSpec(memory_space=pl.ANY),
        out_shape=jax.ShapeDtypeStruct((n_rows_total * nb, LANES), F32),
        scratch_shapes=[pltpu.VMEM((MOE_ROWS // 2 * nb, LANES), F32), pltpu.SemaphoreType.DMA,
                        pltpu.SemaphoreType.DMA],
        compiler_params=_cparams(("arbitrary",)),
        name="dispatch",
    )(dst_flat, counts_i32, x1_slab)


def _moe_kernel(be_ref, nu_ref, xs_ref, wgu_ref, bgu_ref, wd_ref, bd_ref, y_ref, xb):
    dff, d = wd_ref.shape
    nb = d // LANES

    @pl.when(pl.program_id(0) < nu_ref[0])
    def _():
        for j in range(nb):
            xb[:, j * LANES:(j + 1) * LANES] = _load_slab_block(xs_ref, j, MOE_ROWS, nb).astype(BF16)
        hid = _dot(xb[...], wgu_ref[...]) + bgu_ref[...]
        glu = jnp.minimum(hid[:, :dff], SWIGLU_LIMIT)
        lin = jnp.clip(hid[:, dff:], -SWIGLU_LIMIT, SWIGLU_LIMIT)
        act = glu * _sigmoid(SWIGLU_ALPHA * glu) * (lin + 1.0)
        _store_slabs(y_ref, _dot(act.astype(BF16), wd_ref[...]) + bd_ref[...])

    @pl.when(pl.program_id(0) >= nu_ref[0])
    def _():
        y_ref[...] = jnp.zeros_like(y_ref)


def _moe(block_expert, n_used, xs_slab2d, wgu, bgu, wd, bd):
    dff, d = wd.shape[1:]
    nb = d // LANES
    p = xs_slab2d.shape[0] // nb
    n_blocks = p // MOE_ROWS

    def rows(i, be, nu):
        return (jnp.minimum(i, nu[0] - 1), 0)

    def per_expert(i, be, nu):
        return (be[i], 0, 0)

    grid_spec = pltpu.PrefetchScalarGridSpec(
        num_scalar_prefetch=2,
        grid=(n_blocks,),
        in_specs=[pl.BlockSpec((MOE_ROWS * nb, LANES), rows),
                  pl.BlockSpec((None, d, 2 * dff), per_expert),
                  pl.BlockSpec((None, 1, 2 * dff), per_expert),
                  pl.BlockSpec((None, dff, d), per_expert),
                  pl.BlockSpec((None, 1, d), per_expert)],
        out_specs=pl.BlockSpec((MOE_ROWS * nb, LANES), lambda i, be, nu: (i, 0)),
        scratch_shapes=[pltpu.VMEM((MOE_ROWS, d), BF16)],
    )
    return pl.pallas_call(
        _moe_kernel,
        grid_spec=grid_spec,
        out_shape=jax.ShapeDtypeStruct((p * nb, LANES), F32),
        compiler_params=_cparams(("arbitrary",)),
        name="experts",
    )(block_expert, n_used, xs_slab2d, wgu, bgu, wd, bd)


def _combine_kernel(dst_ref, x1_ref, route_ref, lg_ref, lb_ref, y_ref, o_ref, buf, sem):
    tt, d = o_ref.shape
    nb = d // LANES

    def issue(t, _):
        for k in range(TOP_K):
            dd = dst_ref[TOP_K * t + k]
            pltpu.make_async_copy(y_ref.at[_slab_rows(dd, nb)], buf.at[k, _slab_rows(t, nb)], sem).start()
        return 0

    lax.fori_loop(0, tt, issue, 0)
    for k in range(TOP_K):
        pltpu.make_async_copy(y_ref.at[pl.ds(0, tt * nb)], buf.at[k], sem).wait()
    route = route_ref[...]
    lane = _iota(route.shape, 1)
    gates = [jnp.sum(jnp.where(lane == 2 * TOP_K + k, route, 0.0), axis=1, keepdims=True)
             for k in range(TOP_K)]
    blocks = []
    for j in range(nb):
        r = DEEPNORM_ALPHA * _load_slab_block(x1_ref, j, tt, nb)
        for k in range(TOP_K):
            r = r + gates[k] * _load_slab_block(buf.at[k], j, tt, nb)
        blocks.append(r)
    mu = sum(jnp.sum(r, axis=1, keepdims=True) for r in blocks) / d
    var = sum(jnp.sum((r - mu) * (r - mu), axis=1, keepdims=True) for r in blocks) / d
    inv = lax.rsqrt(var + LN_EPS)
    for j in range(nb):
        cols = slice(j * LANES, (j + 1) * LANES)
        o_ref[:, cols] = (blocks[j] - mu) * inv * lg_ref[:, cols] + lb_ref[:, cols]


def _combine(dst_flat, x1_slab, route, lg, lb, y_slab, tt=256):
    t = route.shape[0]
    tt = min(tt, t)
    nb = x1_slab.shape[0] // t
    d = nb * LANES
    return pl.pallas_call(
        _combine_kernel,
        grid=(t // tt,),
        in_specs=[pl.BlockSpec((tt * TOP_K,), lambda i: (i,), memory_space=pltpu.SMEM),
                  pl.BlockSpec((tt * nb, LANES), lambda i: (i, 0)),
                  pl.BlockSpec((tt, LANES), lambda i: (i, 0)),
                  pl.BlockSpec((1, d), lambda i: (0, 0)),
                  pl.BlockSpec((1, d), lambda i: (0, 0)),
                  pl.BlockSpec(memory_space=pl.ANY)],
        out_specs=pl.BlockSpec((tt, d), lambda i: (i, 0)),
        out_shape=jax.ShapeDtypeStruct((t, d), F32),
        scratch_shapes=[pltpu.VMEM((TOP_K, tt * nb, LANES), F32), pltpu.SemaphoreType.DMA],
        compiler_params=_cparams(("arbitrary",)),
        name="combine",
    )(dst_flat, x1_slab, route, lg, lb, y_slab)


def _pad_lanes(a, n=LANES):
    return jnp.pad(a, ((0, 0), (0, n - a.shape[1])))


def kernel(x, mem, w_in, w_conv, a_log, dt_bias, gdn_norm_w, w_mem_kv, w_p_gdn, w_p_sb, w_p_mem, w_o, ln1_g, ln1_b, w_router, b_router, w_gate_up, b_gate_up, w_down, b_down, ln2_g, ln2_b):
    b, s, d = x.shape
    t = b * s
    l = 0
    gdn_w = GDN_HEADS * GDN_DK
    sb_w = SB_HEADS * SB_DH
    mem_w = MEM_HEADS * MEM_DH
    ab0 = 4 * gdn_w
    ab1 = ab0 + 2 * GDN_HEADS
    w = w_in[l]
    w_main = jnp.concatenate([w[:, :ab0], w[:, ab1:]], axis=1).astype(BF16)
    w_ab = w[:, ab0:ab1].astype(BF16)
    x2d = x.reshape(t, d)
    proj, gab, gabt = _proj(x2d, w_main, _pad_lanes(w_ab), jnp.pad(w_ab.T, ((0, 16 - 2 * GDN_HEADS), (0, 0))))
    proj3d = proj.reshape(b, s, -1)
    cb_gq, cb_gk, cb_gv, cb_gz = 0, gdn_w // LANES, 2 * gdn_w // LANES, 3 * gdn_w // LANES
    cb_sq = ab0 // LANES
    cb_sk = cb_sq + sb_w // LANES
    cb_sv = cb_sk + sb_w // LANES
    off_mq = ab0 + 3 * sb_w
    off_gates = off_mq + mem_w

    y_gdn = _gdn(a_log[l], dt_bias[l], proj3d, gab.reshape(b, s, LANES), gabt, w_conv[l],
                 gdn_norm_w[l].reshape(1, -1), cb_gq, cb_gk, cb_gv, cb_gz)
    y_sb = _sb(proj3d, cb_sq, cb_sk, cb_sv)
    memkv = _mm(mem.reshape(-1, d), w_mem_kv[l].astype(BF16), BF16)
    y_mem = _memattn(proj3d, memkv.reshape(b, mem.shape[1], -1), off_mq // mem_w)

    x1, route, counts = _merge(
        x2d, y_gdn.reshape(t, -1), y_sb.reshape(t, -1), y_mem.reshape(t, -1), proj, off_gates // d,
        w_p_gdn[l].astype(BF16), w_p_sb[l].astype(BF16), w_p_mem[l].astype(BF16), w_o[l].astype(BF16),
        ln1_g[l].reshape(1, d), ln1_b[l].reshape(1, d),
        _pad_lanes(w_router[l]).astype(BF16), _pad_lanes(b_router[l].reshape(1, -1)))

    n_blocks = -(-(t * TOP_K) // MOE_ROWS) + N_EXPERTS
    n_blocks_pad = -(-n_blocks // 8) * 8
    dst, blk, meta = _tables(counts, route, n_blocks_pad)
    dst_flat = dst[:, :TOP_K].reshape(-1)
    xs = _dispatch(dst_flat, counts[0, :N_EXPERTS].astype(jnp.int32), x1, t, n_blocks * MOE_ROWS)
    y_pad = _moe(blk[:n_blocks, 0], meta[0, :1], xs,
                 w_gate_up[l].astype(BF16), b_gate_up[l][:, None, :],
                 w_down[l].astype(BF16), b_down[l][:, None, :])
    out = _combine(dst_flat, x1, route, ln2_g[l].reshape(1, d), ln2_b[l].reshape(1, d), y_pad)
    return out.reshape(b, s, d)
```

```python
import functools

import jax
import jax.numpy as jnp
from jax import lax
from jax.experimental import pallas as pl
from jax.experimental.pallas import tpu as pltpu

F32 = jnp.float32
BF16 = jnp.bfloat16

GDN_HEADS = 4
GDN_DK = 128
GDN_CONV = 4
GDN_CHUNK = 64
SB_HEADS = 8
SB_DH = 64
MEM_HEADS = 4
MEM_DH = 128
N_BRANCHES = 3
N_EXPERTS = 32
TOP_K = 4
SWIGLU_LIMIT = 7.0
SWIGLU_ALPHA = 1.702
LN_EPS = 1e-5
NORM_EPS = 1e-6
DEPTH = 1
DEEPNORM_ALPHA = (2 * DEPTH) ** 0.25

LANES = 128
MOE_ROWS = 256
VMEM_LIMIT = 56 * 1024 * 1024

NT = (((1,), (1,)), ((), ()))
TN = (((0,), (0,)), ((), ()))


def _cparams(sem):
    return pltpu.CompilerParams(dimension_semantics=sem, vmem_limit_bytes=VMEM_LIMIT)


def _dot(a, b, dims=None, precision=None):
    if dims is None:
        return jnp.dot(a, b, preferred_element_type=F32, precision=precision)
    return lax.dot_general(a, b, dims, preferred_element_type=F32, precision=precision)


def _iota(shape, dim):
    return lax.broadcasted_iota(jnp.int32, shape, dim)


def _softplus(z):
    return jnp.maximum(z, 0.0) + jnp.log1p(jnp.exp(-jnp.abs(z)))


def _sigmoid(z):
    return 1.0 / (1.0 + jnp.exp(-z))


def _proj_kernel(x_ref, w_ref, wab_ref, wabt_ref, o_ref, oab_ref, oabt_ref, *, n_chunk):
    xb = x_ref[...].astype(BF16)
    n = w_ref.shape[1]
    for j in range(0, n, n_chunk):
        o_ref[:, j:j + n_chunk] = _dot(xb, w_ref[:, j:j + n_chunk]).astype(o_ref.dtype)
    oab_ref[...] = _dot(xb, wab_ref[...])
    oabt_ref[...] = _dot(wabt_ref[...], xb, NT)


def _proj(x2d, w_main, w_ab, w_abt, tm=256, n_chunk=1024):
    t, d = x2d.shape
    tm = min(tm, t)
    n = w_main.shape[1]
    return pl.pallas_call(
        functools.partial(_proj_kernel, n_chunk=n_chunk),
        grid=(t // tm,),
        in_specs=[
            pl.BlockSpec((tm, d), lambda i: (i, 0)),
            pl.BlockSpec((d, n), lambda i: (0, 0), pipeline_mode=pl.Buffered(1)),
            pl.BlockSpec((d, LANES), lambda i: (0, 0), pipeline_mode=pl.Buffered(1)),
            pl.BlockSpec((16, d), lambda i: (0, 0), pipeline_mode=pl.Buffered(1)),
        ],
        out_specs=[
            pl.BlockSpec((tm, n), lambda i: (i, 0)),
            pl.BlockSpec((tm, LANES), lambda i: (i, 0)),
            pl.BlockSpec((16, tm), lambda i: (0, i)),
        ],
        out_shape=[
            jax.ShapeDtypeStruct((t, n), BF16),
            jax.ShapeDtypeStruct((t, LANES), F32),
            jax.ShapeDtypeStruct((16, t), F32),
        ],
        compiler_params=_cparams(("arbitrary",)),
        name="proj",
    )(x2d, w_main, w_ab, w_abt)


def _mm_kernel(x_ref, w_ref, o_ref):
    o_ref[...] = _dot(x_ref[...].astype(BF16), w_ref[...]).astype(o_ref.dtype)


def _mm(x2d, w, out_dtype, tm=512):
    t, d = x2d.shape
    tm = min(tm, t)
    n = w.shape[1]
    return pl.pallas_call(
        _mm_kernel,
        grid=(t // tm,),
        in_specs=[pl.BlockSpec((tm, d), lambda i: (i, 0)),
                  pl.BlockSpec((d, n), lambda i: (0, 0), pipeline_mode=pl.Buffered(1))],
        out_specs=pl.BlockSpec((tm, n), lambda i: (i, 0)),
        out_shape=jax.ShapeDtypeStruct((t, n), out_dtype),
        compiler_params=_cparams(("arbitrary",)),
        name="memkv",
    )(x2d, w)


def _memattn_kernel(q_ref, kv_ref, o_ref):
    w = MEM_HEADS * MEM_DH
    scale = MEM_DH ** -0.5
    for h in range(MEM_HEADS):
        q = q_ref[:, h * MEM_DH:(h + 1) * MEM_DH]
        k = kv_ref[:, h * MEM_DH:(h + 1) * MEM_DH]
        v = kv_ref[:, w + h * MEM_DH:w + (h + 1) * MEM_DH]
        s = _dot(q, k, NT) * scale
        p = jnp.exp(s - jnp.max(s, axis=1, keepdims=True))
        l = jnp.sum(p, axis=1, keepdims=True)
        o = _dot(p.astype(BF16), v) / l
        o_ref[:, h * MEM_DH:(h + 1) * MEM_DH] = o.astype(o_ref.dtype)


def _memattn(proj3d, memkv3d, q_col_block, tq=512):
    b, s, _ = proj3d.shape
    tq = min(tq, s)
    m = memkv3d.shape[1]
    w = MEM_HEADS * MEM_DH
    return pl.pallas_call(
        _memattn_kernel,
        grid=(b, s // tq),
        in_specs=[pl.BlockSpec((None, tq, w), lambda i, j: (i, j, q_col_block)),
                  pl.BlockSpec((None, m, 2 * w), lambda i, j: (i, 0, 0))],
        out_specs=pl.BlockSpec((None, tq, w), lambda i, j: (i, j, 0)),
        out_shape=jax.ShapeDtypeStruct((b, s, w), BF16),
        compiler_params=_cparams(("arbitrary", "arbitrary")),
        name="memattn",
    )(proj3d, memkv3d)


def _sb_kernel(q_ref, k_ref, v_ref, o_ref, *, blk):
    qi = pl.program_id(2)
    lane = _iota((blk, LANES), 1)
    first = lane < SB_DH
    qs = (q_ref[...].astype(F32) * (SB_DH ** -0.5)).astype(BF16)
    zero = jnp.zeros_like(qs)
    q2 = jnp.concatenate([jnp.where(first, qs, zero), jnp.where(first, zero, qs)], axis=0)
    row = _iota((2 * blk, blk), 0)
    col = _iota((2 * blk, blk), 1)
    causal = col < jnp.where(row >= blk, row - blk, row)
    upper = (_iota((blk, blk), 0) > _iota((blk, blk), 1)).astype(BF16)

    def block(j, carry, acc, diag):
        r0 = pl.multiple_of(j * blk, blk)
        z = _dot(q2, k_ref[pl.ds(r0, blk), :], NT)
        sp = _softplus(z)
        if diag:
            sp = jnp.where(causal, sp, 0.0)
        after = _dot(sp.astype(BF16), upper)
        w = jnp.exp(((z - sp) - after) - carry)
        if diag:
            w = jnp.where(causal, w, 0.0)
        acc = acc + _dot(w.astype(BF16), v_ref[pl.ds(r0, blk), :])
        return carry + jnp.sum(sp, axis=1, keepdims=True), acc

    carry, acc = block(qi, jnp.zeros((2 * blk, 1), F32), jnp.zeros((2 * blk, LANES), F32), True)
    carry, acc = lax.fori_loop(0, qi, lambda i, c: block(qi - 1 - i, c[0], c[1], False), (carry, acc))
    o_ref[...] = jnp.where(first, acc[:blk], acc[blk:]).astype(o_ref.dtype)


def _sb(proj3d, q_cb, k_cb, v_cb, blk=256):
    b, s, _ = proj3d.shape
    pairs = SB_HEADS * SB_DH // LANES
    return pl.pallas_call(
        functools.partial(_sb_kernel, blk=blk),
        grid=(b, pairs, s // blk),
        in_specs=[pl.BlockSpec((None, blk, LANES), lambda i, p, j: (i, j, q_cb + p)),
                  pl.BlockSpec((None, s, LANES), lambda i, p, j: (i, 0, k_cb + p)),
                  pl.BlockSpec((None, s, LANES), lambda i, p, j: (i, 0, v_cb + p))],
        out_specs=pl.BlockSpec((None, blk, LANES), lambda i, p, j: (i, j, p)),
        out_shape=jax.ShapeDtypeStruct((b, s, pairs * LANES), BF16),
        compiler_params=_cparams(("arbitrary", "arbitrary", "arbitrary")),
        name="stickbreak",
    )(proj3d, proj3d, proj3d)


GDN_PREP_GROUP = 8


def _gdn_kernel(alog_ref, dtb_ref, q_ref, k_ref, v_ref, z_ref, gab_ref, gabt_ref,
                wq_ref, wk_ref, wv_ref, nw_ref, y_ref,
                qc, kc, vc, gcol, bcol, grow, u_s, w_s, qg_s, kd_s, at_s, gl_s):
    h = pl.program_id(1)
    s = q_ref.shape[0]
    c = GDN_CHUNK
    r = LANES
    n_tiles = s // r

    def conv_silu(x_ref, w_ref):
        x = x_ref[...].astype(F32)
        rowi = _iota(x.shape, 0)
        y = x * w_ref[GDN_CONV - 1:GDN_CONV, :]
        for d in range(1, GDN_CONV):
            xs = jnp.where(rowi >= d, pltpu.roll(x, d, 0), 0.0)
            y = y + xs * w_ref[GDN_CONV - 1 - d:GDN_CONV - d, :]
        return y * _sigmoid(y)

    def l2norm(t):
        return t * lax.rsqrt(jnp.sum(t * t, axis=1, keepdims=True) + NORM_EPS)

    qc[...] = l2norm(conv_silu(q_ref, wq_ref)) * (GDN_DK ** -0.5)
    kc[...] = l2norm(conv_silu(k_ref, wk_ref))
    vc[...] = conv_silu(v_ref, wv_ref)

    a_neg = -jnp.exp(jnp.full((1, 1), alog_ref[h], F32))
    dtb = dtb_ref[h]
    lane = _iota((s, LANES), 1)
    gab = gab_ref[...]
    ga = jnp.sum(jnp.where(lane == h, gab, 0.0), axis=1, keepdims=True)
    gb = jnp.sum(jnp.where(lane == h + GDN_HEADS, gab, 0.0), axis=1, keepdims=True)
    gcol[...] = jnp.broadcast_to(a_neg * _softplus(ga + dtb), (s, LANES))
    bcol[...] = jnp.broadcast_to(_sigmoid(gb), (s, LANES))
    g_row = a_neg * _softplus(gabt_ref[pl.ds(h, 1), :] + dtb)
    for i in range(n_tiles):
        grow[i] = jnp.broadcast_to(g_row[:, i * r:(i + 1) * r], (16, r))

    row = _iota((r, r), 0)
    col = _iota((r, r), 1)
    same = (row // c) == (col // c)
    m_incl = same & (col <= row)
    m_strict = same & (col < row)
    l_incl = m_incl.astype(BF16)
    l_incl_t = (same & (row <= col)).astype(BF16)
    l_same = same.astype(BF16)

    group = GDN_PREP_GROUP if n_tiles % GDN_PREP_GROUP == 0 else 1
    tiles = range(group)

    def prep(it, _):
        idx = [it * group + t for t in tiles]
        r0 = [pl.multiple_of(i * r, r) for i in idx]
        rows = [pl.ds(x, r) for x in r0]
        q = [qc[rw, :] for rw in rows]
        k = [kc[rw, :] for rw in rows]
        v = [vc[rw, :] for rw in rows]
        g = [gcol[rw, :] for rw in rows]
        beta = [bcol[rw, :] for rw in rows]
        g_hi = [g[t].astype(BF16) for t in tiles]
        g_lo = [(g[t] - g_hi[t].astype(F32)).astype(BF16) for t in tiles]
        gr = [grow[idx[t]] for t in tiles]
        gr_hi = [gr[t].astype(BF16) for t in tiles]
        gr_lo = [(gr[t] - gr_hi[t].astype(F32)).astype(BF16) for t in tiles]
        gc = [_dot(l_incl, g_hi[t]) + _dot(l_incl, g_lo[t]) for t in tiles]
        gc_row = [(_dot(gr_hi[t], l_incl_t) + _dot(gr_lo[t], l_incl_t))[0:1, :] for t in tiles]
        glast = [_dot(l_same, g_hi[t]) + _dot(l_same, g_lo[t]) for t in tiles]
        decay = [jnp.where(m_incl, jnp.exp(jnp.where(m_incl, gc[t] - gc_row[t], 0.0)), 0.0) for t in tiles]
        kb = [k[t].astype(BF16) for t in tiles]
        kk = [_dot(kb[t], kb[t], NT) for t in tiles]
        m_pow = [-jnp.where(m_strict, kk[t] * decay[t] * beta[t], 0.0) for t in tiles]
        t_off = list(m_pow)
        for _unused in range(5):
            mb = [m.astype(BF16) for m in m_pow]
            m_pow = [_dot(mb[t], mb[t]) for t in tiles]
            t_off = [t_off[t] + m_pow[t] + _dot(m_pow[t].astype(BF16), t_off[t].astype(BF16)) for t in tiles]
        eg = [jnp.exp(gc[t]) for t in tiles]
        rhs = [jnp.concatenate([v[t] * beta[t], k[t] * (beta[t] * eg[t])], axis=1) for t in tiles]
        uw = [rhs[t] + _dot(t_off[t].astype(BF16), rhs[t].astype(BF16)) for t in tiles]
        attn = [_dot(q[t].astype(BF16), kb[t], NT) * decay[t] for t in tiles]
        for t in tiles:
            u_s[h, rows[t], :] = uw[t][:, :LANES]
            w_s[h, rows[t], :] = uw[t][:, LANES:].astype(BF16)
            qg_s[h, rows[t], :] = (q[t] * eg[t]).astype(BF16)
            kd_s[h, rows[t], :] = (k[t] * jnp.exp(glast[t] - gc[t])).astype(BF16)
            at_s[h, pl.ds(r0[t], c), :] = attn[t][0:c, 0:c].astype(BF16)
            at_s[h, pl.ds(r0[t] + c, c), :] = attn[t][c:2 * c, c:2 * c].astype(BF16)
            gl_s[h, 2 * idx[t]] = glast[t][0:8, :]
            gl_s[h, 2 * idx[t] + 1] = glast[t][c:c + 8, :]
        return 0

    lax.fori_loop(0, n_tiles // group, prep, 0)

    @pl.when(h == GDN_HEADS - 1)
    def _():
        nw = nw_ref[...]
        heads = range(GDN_HEADS)

        def chunk(i, states):
            rows = pl.ds(pl.multiple_of(i * c, c), c)
            sb = [states[hd].astype(BF16) for hd in heads]
            ws = [_dot(w_s[hd, rows, :], sb[hd]) for hd in heads]
            qs = [_dot(qg_s[hd, rows, :], sb[hd]) for hd in heads]
            vb = [(u_s[hd, rows, :] - ws[hd]).astype(BF16) for hd in heads]
            o = [qs[hd] + _dot(at_s[hd, rows, :], vb[hd]) for hd in heads]
            upd = [_dot(kd_s[hd, rows, :], vb[hd], TN) for hd in heads]
            for hd in heads:
                cols = slice(hd * LANES, (hd + 1) * LANES)
                on = o[hd] * lax.rsqrt(jnp.mean(o[hd] * o[hd], axis=1, keepdims=True) + NORM_EPS) * nw
                zg = z_ref[rows, cols].astype(F32)
                y_ref[rows, cols] = (on * (zg * _sigmoid(zg))).astype(y_ref.dtype)
            return tuple(states[hd] * jnp.exp(gl_s[hd, i][0:1, :]) + upd[hd] for hd in heads)

        zero = jnp.zeros((GDN_DK, LANES), F32)
        lax.fori_loop(0, s // c, chunk, (zero,) * GDN_HEADS)


def _gdn(a_log, dt_bias, proj3d, gab3d, gabt, w_conv, norm_w, q_cb, k_cb, v_cb, z_cb):
    b, s, _ = proj3d.shape
    hh = GDN_HEADS
    hw = hh * LANES
    smem = pl.BlockSpec(memory_space=pltpu.SMEM)

    def col(cb):
        return pl.BlockSpec((None, s, LANES), lambda i, h: (i, 0, cb + h))

    def wcol(cb):
        return pl.BlockSpec((GDN_CONV, LANES), lambda i, h: (0, cb + h))

    def per_head(shape, dtype):
        return pltpu.VMEM((hh,) + shape, dtype)

    return pl.pallas_call(
        _gdn_kernel,
        grid=(b, hh),
        in_specs=[smem, smem, col(q_cb), col(k_cb), col(v_cb),
                  pl.BlockSpec((None, s, hw), lambda i, h: (i, 0, z_cb * LANES // hw)),
                  pl.BlockSpec((None, s, LANES), lambda i, h: (i, 0, 0)),
                  pl.BlockSpec((16, s), lambda i, h: (0, i)),
                  wcol(0), wcol(hh), wcol(2 * hh),
                  pl.BlockSpec((1, LANES), lambda i, h: (0, 0))],
        out_specs=pl.BlockSpec((None, s, hw), lambda i, h: (i, 0, 0)),
        out_shape=jax.ShapeDtypeStruct((b, s, hw), BF16),
        scratch_shapes=[
            pltpu.VMEM((s, LANES), F32), pltpu.VMEM((s, LANES), F32), pltpu.VMEM((s, LANES), F32),
            pltpu.VMEM((s, LANES), F32), pltpu.VMEM((s, LANES), F32),
            pltpu.VMEM((s // LANES, 16, LANES), F32),
            per_head((s, LANES), F32), per_head((s, LANES), BF16),
            per_head((s, LANES), BF16), per_head((s, LANES), BF16),
            per_head((s, GDN_CHUNK), BF16),
            per_head((s // GDN_CHUNK, 8, LANES), F32),
        ],
        compiler_params=_cparams(("arbitrary", "arbitrary")),
        name="gdn",
    )(a_log, dt_bias, proj3d, proj3d, proj3d, proj3d, gab3d, gabt, w_conv, w_conv, w_conv, norm_w)


def _store_slabs(ref, val):
    n, d = val.shape
    nb = d // LANES
    for j in range(nb):
        ref[pl.ds(j, n, stride=nb), :] = val[:, j * LANES:(j + 1) * LANES]


def _load_slab_block(ref, j, n, nb):
    return ref[pl.ds(j, n, stride=nb), :]


def _slab_rows(t, nb):
    return pl.ds(pl.multiple_of(t * nb, nb), nb)


def _layer_norm(r, g, b):
    mu = jnp.mean(r, axis=1, keepdims=True)
    d = r - mu
    var = jnp.mean(d * d, axis=1, keepdims=True)
    return d * lax.rsqrt(var + LN_EPS) * g + b


def _merge_kernel(x_ref, yg_ref, ys_ref, ym_ref, g0_ref, g1_ref, g2_ref, wg_ref, ws_ref, wm_ref,
                  wo_ref, lg_ref, lb_ref, wr_ref, br_ref, x1_ref, route_ref, cnt_ref, carry):
    i = pl.program_id(0)
    tm = x_ref.shape[0]

    @pl.when(i == 0)
    def _():
        carry[...] = jnp.zeros_like(carry)

    def branch(y_ref, w_ref, g_ref):
        return _sigmoid(g_ref[...].astype(F32)) * _dot(y_ref[...], w_ref[...])

    mixed = branch(yg_ref, wg_ref, g0_ref) + branch(ys_ref, ws_ref, g1_ref) + branch(ym_ref, wm_ref, g2_ref)
    r = DEEPNORM_ALPHA * x_ref[...] + _dot(mixed.astype(BF16), wo_ref[...])
    x1 = _layer_norm(r, lg_ref[...], lb_ref[...])
    _store_slabs(x1_ref, x1)

    lane = _iota((tm, LANES), 1)
    lanef = lane.astype(F32)
    neg = jnp.float32(-jnp.inf)
    logits = jnp.where(lane < N_EXPERTS, _dot(x1.astype(BF16), wr_ref[...]) + br_ref[...], neg)
    vals, idxs = [], []
    for _unused in range(TOP_K):
        m = jnp.max(logits, axis=1, keepdims=True)
        idx = jnp.min(jnp.where(logits == m, lanef, float(LANES)), axis=1, keepdims=True)
        vals.append(m)
        idxs.append(idx)
        logits = jnp.where(lanef == idx, neg, logits)
    es = [jnp.exp(v - vals[0]) for v in vals]
    den = es[0] + es[1] + es[2] + es[3]
    onehot = jnp.zeros((tm, LANES), F32)
    for idx in idxs:
        onehot = onehot + (lanef == idx).astype(F32)
    row = _iota((tm, tm), 0)
    col = _iota((tm, tm), 1)
    before = _dot((col < row).astype(BF16), onehot.astype(BF16)) + carry[0:1, :]
    route = jnp.zeros((tm, LANES), F32)
    for k in range(TOP_K):
        rank = jnp.sum(jnp.where(lanef == idxs[k], before, 0.0), axis=1, keepdims=True)
        route = jnp.where(lane == k, idxs[k], route)
        route = jnp.where(lane == TOP_K + k, rank, route)
        route = jnp.where(lane == 2 * TOP_K + k, es[k] / den, route)
    route_ref[...] = route
    carry[...] = carry[...] + jnp.sum(onehot, axis=0, keepdims=True)
    cnt_ref[...] = carry[...]


def _merge(x2d, yg, ys, ym, proj, gate_cb, wg, ws, wm, wo, lg, lb, wr, br, tm=256):
    t, d = x2d.shape
    tm = min(tm, t)
    wb = yg.shape[1]

    def rows(w):
        return pl.BlockSpec((tm, w), lambda i: (i, 0))

    def res(shape):
        return pl.BlockSpec(shape, lambda i: (0, 0), pipeline_mode=pl.Buffered(1))

    def gate(j):
        return pl.BlockSpec((tm, d), lambda i: (i, gate_cb + j))

    return pl.pallas_call(
        _merge_kernel,
        grid=(t // tm,),
        in_specs=[rows(d), rows(wb), rows(wb), rows(wb), gate(0), gate(1), gate(2),
                  res((wb, d)), res((wb, d)), res((wb, d)), res((d, d)),
                  res((1, d)), res((1, d)), res((d, LANES)), res((1, LANES))],
        out_specs=[pl.BlockSpec((tm * (d // LANES), LANES), lambda i: (i, 0)), rows(LANES),
                   pl.BlockSpec((8, LANES), lambda i: (0, 0))],
        out_shape=[jax.ShapeDtypeStruct((t * (d // LANES), LANES), F32),
                   jax.ShapeDtypeStruct((t, LANES), F32), jax.ShapeDtypeStruct((8, LANES), F32)],
        scratch_shapes=[pltpu.VMEM((8, LANES), F32)],
        compiler_params=_cparams(("arbitrary",)),
        name="merge_route",
    )(x2d, yg, ys, ym, proj, proj, proj, wg, ws, wm, wo, lg, lb, wr, br)


def _tables_kernel(cnt_ref, route_ref, dst_ref, blk_ref, meta_ref, *, n_blocks):
    lane8 = _iota((8, LANES), 1)
    cnt = jnp.where(lane8 < N_EXPERTS, cnt_ref[...], 0.0)
    nblk = jnp.floor((cnt + (MOE_ROWS - 1)) * (1.0 / MOE_ROWS))
    row = _iota((LANES, LANES), 0)
    col = _iota((LANES, LANES), 1)
    end_blk = _dot(nblk.astype(BF16), (row <= col).astype(BF16))
    start_row = (end_blk - nblk) * MOE_ROWS
    start1 = start_row[0:1, :]
    end1 = end_blk[0:1, :]

    @pl.when(pl.program_id(0) == 0)
    def _():
        bi = _iota((n_blocks, LANES), 0).astype(F32)
        lane = _iota((n_blocks, LANES), 1)
        passed = jnp.where((lane < N_EXPERTS) & (end1 <= bi), 1.0, 0.0)
        be = jnp.minimum(jnp.sum(passed, axis=1, keepdims=True), N_EXPERTS - 1.0)
        blk_ref[...] = jnp.broadcast_to(be, (n_blocks, LANES)).astype(jnp.int32)
        meta = jnp.where(lane8 == 0, jnp.max(end_blk, axis=1, keepdims=True), 0.0)
        meta_ref[...] = meta.astype(jnp.int32)

    route = route_ref[...]
    tm = route.shape[0]
    lane = _iota((tm, LANES), 1)
    lanef = lane.astype(F32)
    out = jnp.zeros((tm, LANES), F32)
    for k in range(TOP_K):
        idx = jnp.sum(jnp.where(lane == k, route, 0.0), axis=1, keepdims=True)
        rank = jnp.sum(jnp.where(lane == TOP_K + k, route, 0.0), axis=1, keepdims=True)
        base = jnp.sum(jnp.where(lanef == idx, start1, 0.0), axis=1, keepdims=True)
        out = jnp.where(lane == k, base + rank, out)
    dst_ref[...] = out.astype(jnp.int32)


def _tables(counts, route, n_blocks, tm=1024):
    t = route.shape[0]
    tm = min(tm, t)
    return pl.pallas_call(
        functools.partial(_tables_kernel, n_blocks=n_blocks),
        grid=(t // tm,),
        in_specs=[pl.BlockSpec((8, LANES), lambda i: (0, 0)),
                  pl.BlockSpec((tm, LANES), lambda i: (i, 0))],
        out_specs=[pl.BlockSpec((tm, LANES), lambda i: (i, 0)),
                   pl.BlockSpec((n_blocks, LANES), lambda i: (0, 0)),
                   pl.BlockSpec((8, LANES), lambda i: (0, 0))],
        out_shape=[jax.ShapeDtypeStruct((t, LANES), jnp.int32),
                   jax.ShapeDtypeStruct((n_blocks, LANES), jnp.int32),
                   jax.ShapeDtypeStruct((8, LANES), jnp.int32)],
        compiler_params=_cparams(("arbitrary",)),
        name="route_tables",
    )(counts, route)


def _dispatch_kernel(dst_ref, cnt_ref, x_ref, xs_ref, zbuf, sem, zsem, *, nb):
    tt = x_ref.shape[0] // nb

    @pl.when(pl.program_id(0) == 0)
    def _():
        zbuf[...] = jnp.zeros_like(zbuf)

        def fill_expert(e, start):
            cnt = cnt_ref[e]
            padded = ((cnt + (MOE_ROWS - 1)) // MOE_ROWS) * MOE_ROWS
            n_pad = padded - cnt
            bit = MOE_ROWS // 2
            while bit >= 1:
                pos = start + cnt + (n_pad & ~(2 * bit - 1))

                @pl.when((n_pad & bit) != 0)
                def _(pos=pos, bit=bit):
                    cp = pltpu.make_async_copy(zbuf.at[pl.ds(0, bit * nb)],
                                               xs_ref.at[pl.ds(pl.multiple_of(pos * nb, nb), bit * nb)], zsem)
                    cp.start()
                    cp.wait()

                bit //= 2
            return start + padded

        used = lax.fori_loop(0, N_EXPERTS, fill_expert, 0)
        zrows = zbuf.shape[0] // nb

        def fill_tail(i, _):
            pos = used + i * zrows
            cp = pltpu.make_async_copy(zbuf, xs_ref.at[pl.ds(pl.multiple_of(pos * nb, nb), zrows * nb)], zsem)
            cp.start()
            cp.wait()
            return 0

        lax.fori_loop(0, (xs_ref.shape[0] // nb - used) // zrows, fill_tail, 0)

    def issue(t, _):
        for k in range(TOP_K):
            d = dst_ref[TOP_K * t + k]
            pltpu.make_async_copy(x_ref.at[_slab_rows(t, nb)], xs_ref.at[_slab_rows(d, nb)],
                                  sem).start(priority=k % 2)
        return 0

    lax.fori_loop(0, tt, issue, 0)
    for k in range(TOP_K):
        pltpu.make_async_copy(x_ref, xs_ref.at[pl.ds(0, tt * nb)], sem).wait()


def _dispatch(dst_flat, counts_i32, x1_slab, t, n_rows_total, tt=256):
    nb = x1_slab.shape[0] // t
    tt = min(tt, t)
    return pl.pallas_call(
        functools.partial(_dispatch_kernel, nb=nb),
        grid=(t // tt,),
        in_specs=[pl.BlockSpec((tt * TOP_K,), lambda i: (i,), memory_space=pltpu.SMEM),
                  pl.BlockSpec(memory_space=pltpu.SMEM),
                  pl.BlockSpec((tt * nb, LANES), lambda i: (i, 0))],
        out_specs=pl.BlockSpec(memory_space=pl.ANY),
        out_shape=jax.ShapeDtypeStruct((n_rows_total * nb, LANES), F32),
        scratch_shapes=[pltpu.VMEM((MOE_ROWS // 2 * nb, LANES), F32), pltpu.SemaphoreType.DMA,
                        pltpu.SemaphoreType.DMA],
        compiler_params=_cparams(("arbitrary",)),
        name="dispatch",
    )(dst_flat, counts_i32, x1_slab)


def _moe_kernel(be_ref, nu_ref, xs_ref, wgu_ref, bgu_ref, wd_ref, bd_ref, y_ref, xb, wgu_b, wd_b):
    dff, d = wd_ref.shape
    nb = d // LANES
    i = pl.program_id(0)

    @pl.when((i == 0) | (be_ref[i] != be_ref[jnp.maximum(i - 1, 0)]))
    def _():
        wgu_b[...] = wgu_ref[...].astype(BF16)
        wd_b[...] = wd_ref[...].astype(BF16)

    @pl.when(i < nu_ref[0])
    def _():
        for j in range(nb):
            xb[:, j * LANES:(j + 1) * LANES] = _load_slab_block(xs_ref, j, MOE_ROWS, nb).astype(BF16)
        hid = _dot(xb[...], wgu_b[...]) + bgu_ref[...]
        glu = jnp.minimum(hid[:, :dff], SWIGLU_LIMIT)
        lin = jnp.clip(hid[:, dff:], -SWIGLU_LIMIT, SWIGLU_LIMIT)
        act = glu * _sigmoid(SWIGLU_ALPHA * glu) * (lin + 1.0)
        _store_slabs(y_ref, _dot(act.astype(BF16), wd_b[...]) + bd_ref[...])

    @pl.when(i >= nu_ref[0])
    def _():
        y_ref[...] = jnp.zeros_like(y_ref)


def _moe(block_expert, n_used, xs_slab2d, wgu, bgu, wd, bd):
    dff, d = wd.shape[1:]
    nb = d // LANES
    p = xs_slab2d.shape[0] // nb
    n_blocks = p // MOE_ROWS

    def rows(i, be, nu):
        return (jnp.minimum(i, nu[0] - 1), 0)

    def per_expert(i, be, nu):
        return (be[i], 0, 0)

    grid_spec = pltpu.PrefetchScalarGridSpec(
        num_scalar_prefetch=2,
        grid=(n_blocks,),
        in_specs=[pl.BlockSpec((MOE_ROWS * nb, LANES), rows),
                  pl.BlockSpec((None, d, 2 * dff), per_expert),
                  pl.BlockSpec((None, 1, 2 * dff), per_expert),
                  pl.BlockSpec((None, dff, d), per_expert),
                  pl.BlockSpec((None, 1, d), per_expert)],
        out_specs=pl.BlockSpec((MOE_ROWS * nb, LANES), lambda i, be, nu: (i, 0)),
        scratch_shapes=[pltpu.VMEM((MOE_ROWS, d), BF16), pltpu.VMEM((d, 2 * dff), BF16),
                        pltpu.VMEM((dff, d), BF16)],
    )
    return pl.pallas_call(
        _moe_kernel,
        grid_spec=grid_spec,
        out_shape=jax.ShapeDtypeStruct((p * nb, LANES), F32),
        compiler_params=_cparams(("arbitrary",)),
        name="experts",
    )(block_expert, n_used, xs_slab2d, wgu, bgu, wd, bd)


def _combine_kernel(dst_ref, dstn_ref, x1_ref, route_ref, lg_ref, lb_ref, y_ref, o_ref, buf, sem):
    tt, d = o_ref.shape
    nb = d // LANES
    step = pl.program_id(0)
    slot = step % 2

    def gather(idx_ref, s):
        def issue(t, _):
            for k in range(TOP_K):
                dd = idx_ref[TOP_K * t + k]
                pltpu.make_async_copy(y_ref.at[_slab_rows(dd, nb)], buf.at[s, k, _slab_rows(t, nb)],
                                      sem.at[s]).start(priority=k % 2)
            return 0

        lax.fori_loop(0, tt, issue, 0)

    @pl.when(step == 0)
    def _():
        gather(dst_ref, 0)

    @pl.when(step + 1 < pl.num_programs(0))
    def _():
        gather(dstn_ref, 1 - slot)

    for k in range(TOP_K):
        pltpu.make_async_copy(y_ref.at[pl.ds(0, tt * nb)], buf.at[slot, k], sem.at[slot]).wait()
    buf = buf.at[slot]
    route = route_ref[...]
    lane = _iota(route.shape, 1)
    gates = [jnp.sum(jnp.where(lane == 2 * TOP_K + k, route, 0.0), axis=1, keepdims=True)
             for k in range(TOP_K)]
    blocks = []
    for j in range(nb):
        r = DEEPNORM_ALPHA * _load_slab_block(x1_ref, j, tt, nb)
        for k in range(TOP_K):
            r = r + gates[k] * _load_slab_block(buf.at[k], j, tt, nb)
        blocks.append(r)
    mu = sum(jnp.sum(r, axis=1, keepdims=True) for r in blocks) / d
    var = sum(jnp.sum((r - mu) * (r - mu), axis=1, keepdims=True) for r in blocks) / d
    inv = lax.rsqrt(var + LN_EPS)
    for j in range(nb):
        cols = slice(j * LANES, (j + 1) * LANES)
        o_ref[:, cols] = (blocks[j] - mu) * inv * lg_ref[:, cols] + lb_ref[:, cols]


def _combine(dst_flat, x1_slab, route, lg, lb, y_slab, tt=256):
    t = route.shape[0]
    tt = min(tt, t)
    nb = x1_slab.shape[0] // t
    d = nb * LANES
    last = t // tt - 1
    return pl.pallas_call(
        _combine_kernel,
        grid=(t // tt,),
        in_specs=[pl.BlockSpec((tt * TOP_K,), lambda i: (i,), memory_space=pltpu.SMEM),
                  pl.BlockSpec((tt * TOP_K,), lambda i: (jnp.minimum(i + 1, last),), memory_space=pltpu.SMEM),
                  pl.BlockSpec((tt * nb, LANES), lambda i: (i, 0)),
                  pl.BlockSpec((tt, LANES), lambda i: (i, 0)),
                  pl.BlockSpec((1, d), lambda i: (0, 0)),
                  pl.BlockSpec((1, d), lambda i: (0, 0)),
                  pl.BlockSpec(memory_space=pl.ANY)],
        out_specs=pl.BlockSpec((tt, d), lambda i: (i, 0)),
        out_shape=jax.ShapeDtypeStruct((t, d), F32),
        scratch_shapes=[pltpu.VMEM((2, TOP_K, tt * nb, LANES), F32), pltpu.SemaphoreType.DMA((2,))],
        compiler_params=_cparams(("arbitrary",)),
        name="combine",
    )(dst_flat, dst_flat, x1_slab, route, lg, lb, y_slab)


def _pad_lanes(a, n=LANES):
    return jnp.pad(a, ((0, 0), (0, n - a.shape[1])))


def kernel(x, mem, w_in, w_conv, a_log, dt_bias, gdn_norm_w, w_mem_kv, w_p_gdn, w_p_sb, w_p_mem, w_o, ln1_g, ln1_b, w_router, b_router, w_gate_up, b_gate_up, w_down, b_down, ln2_g, ln2_b):
    b, s, d = x.shape
    t = b * s
    l = 0
    gdn_w = GDN_HEADS * GDN_DK
    sb_w = SB_HEADS * SB_DH
    mem_w = MEM_HEADS * MEM_DH
    ab0 = 4 * gdn_w
    ab1 = ab0 + 2 * GDN_HEADS
    w = w_in[l]
    w_main = jnp.concatenate([w[:, :ab0], w[:, ab1:]], axis=1).astype(BF16)
    w_ab = w[:, ab0:ab1].astype(BF16)
    x2d = x.reshape(t, d)
    proj, gab, gabt = _proj(x2d, w_main, _pad_lanes(w_ab), jnp.pad(w_ab.T, ((0, 16 - 2 * GDN_HEADS), (0, 0))))
    proj3d = proj.reshape(b, s, -1)
    cb_gq, cb_gk, cb_gv, cb_gz = 0, gdn_w // LANES, 2 * gdn_w // LANES, 3 * gdn_w // LANES
    cb_sq = ab0 // LANES
    cb_sk = cb_sq + sb_w // LANES
    cb_sv = cb_sk + sb_w // LANES
    off_mq = ab0 + 3 * sb_w
    off_gates = off_mq + mem_w

    y_gdn = _gdn(a_log[l], dt_bias[l], proj3d, gab.reshape(b, s, LANES), gabt, w_conv[l],
                 gdn_norm_w[l].reshape(1, -1), cb_gq, cb_gk, cb_gv, cb_gz)
    y_sb = _sb(proj3d, cb_sq, cb_sk, cb_sv)
    memkv = _mm(mem.reshape(-1, d), w_mem_kv[l].astype(BF16), BF16)
    y_mem = _memattn(proj3d, memkv.reshape(b, mem.shape[1], -1), off_mq // mem_w)

    x1, route, counts = _merge(
        x2d, y_gdn.reshape(t, -1), y_sb.reshape(t, -1), y_mem.reshape(t, -1), proj, off_gates // d,
        w_p_gdn[l].astype(BF16), w_p_sb[l].astype(BF16), w_p_mem[l].astype(BF16), w_o[l].astype(BF16),
        ln1_g[l].reshape(1, d), ln1_b[l].reshape(1, d),
        _pad_lanes(w_router[l]).astype(BF16), _pad_lanes(b_router[l].reshape(1, -1)))

    n_blocks = -(-(t * TOP_K) // MOE_ROWS) + N_EXPERTS
    n_blocks_pad = -(-n_blocks // 8) * 8
    dst, blk, meta = _tables(counts, route, n_blocks_pad)
    dst_flat = dst[:, :TOP_K].reshape(-1)
    xs = _dispatch(dst_flat, counts[0, :N_EXPERTS].astype(jnp.int32), x1, t, n_blocks * MOE_ROWS)
    y_pad = _moe(blk[:n_blocks, 0], meta[0, :1], xs,
                 w_gate_up[l], b_gate_up[l][:, None, :], w_down[l], b_down[l][:, None, :])
    out = _combine(dst_flat, x1, route, ln2_g[l].reshape(1, d), ln2_b[l].reshape(1, d), y_pad)
    return out.reshape(b, s, d)
```
